```python
import math
import jax, jax.numpy as jnp
from jax import lax
import numpy as np

D_MODEL = 1024
BATCH = 8
SEQ = 2048
DEPTH = 1
DEC_BATCH = 128
DEC_SEQ = 1
PAST_LEN = 2048
PAGE_SIZE = 128

ATT_HEADS = 16
HEAD_DIM = 64
D_ATT = ATT_HEADS * HEAD_DIM
Q_BLOCK = 128
SB_BIAS_INIT = -7.0
SSM_GROUP = 16
D_SSM = D_MODEL
SSM_GROUPS = D_SSM // SSM_GROUP
SSM_STATE = 64
DT_MIN = 1e-3
DT_MAX = 1e-1
LN_EPS = 1e-5
DEEPNORM_ALPHA = (2.0 * DEPTH) ** 0.25
DEEPNORM_BETA = (8.0 * DEPTH) ** -0.25
D_IN = 4 * D_ATT + 2 * D_SSM + 2 * D_MODEL
IN_SPLITS = (D_ATT, 2 * D_ATT, 3 * D_ATT, 4 * D_ATT,
             4 * D_ATT + D_SSM, 4 * D_ATT + 2 * D_SSM,
             4 * D_ATT + 2 * D_SSM + D_MODEL)

kernel_name = "stickbreak_s5_gated_hybrid_step"


def _layer_norm(x, g, b):
    xf = x.astype(jnp.float32)
    mu = xf.mean(-1, keepdims=True)
    var = jnp.square(xf - mu).mean(-1, keepdims=True)
    return ((xf - mu) * lax.rsqrt(var + LN_EPS) * g + b).astype(x.dtype)


def _modulate(x, c, w_cond, b_cond):
    mod = c @ w_cond + b_cond
    shift, scale, gate = jnp.split(mod, 3, axis=-1)
    h = x * (1.0 + scale[:, None]) + shift[:, None]
    return h, gate[:, None]


def _project_in(h, w_in):
    bsz, s = h.shape[:2]
    z = h @ w_in
    q, k, v, g_att, u, g_ssm, m_att, m_ssm = jnp.split(z, IN_SPLITS, axis=-1)
    heads = lambda t: t.reshape(bsz, s, ATT_HEADS, HEAD_DIM)
    return heads(q), heads(k), heads(v), g_att, u, g_ssm, m_att, m_ssm


def _stick_breaking(q, k, v, q_pos, k_pos, sb_bias):
    z = jnp.einsum("bqhd,bkhd->bhqk", q.astype(jnp.float32), k.astype(jnp.float32)) * (HEAD_DIM ** -0.5)
    z = z + sb_bias.astype(jnp.float32)[None, :, None, None]
    causal = (k_pos[None, :] < q_pos[:, None])[None, None]
    log_beta = jax.nn.log_sigmoid(z)
    log_keep = jnp.where(causal, log_beta - z, 0.0)
    log_w = log_beta + lax.cumsum(log_keep, axis=3, reverse=True) - log_keep
    w = jnp.where(causal, jnp.exp(log_w), 0.0)
    return jnp.einsum("bhqk,bkhd->bqhd", w, v.astype(jnp.float32)).astype(v.dtype)


def _sb_prompt(q, k, v, sb_bias):
    b, s, h, d = q.shape
    nb = s // Q_BLOCK
    q_blocks = q.reshape(b, nb, Q_BLOCK, h, d).swapaxes(0, 1)
    k_pos = jnp.arange(s, dtype=jnp.int32)
    starts = jnp.arange(nb, dtype=jnp.int32) * Q_BLOCK

    def one_block(args):
        qb, start = args
        return _stick_breaking(qb, k, v, start + jnp.arange(Q_BLOCK, dtype=jnp.int32), k_pos, sb_bias)

    o = lax.map(one_block, (q_blocks, starts))
    return o.swapaxes(0, 1).reshape(b, s, h, d)


def _ssm_discretize(a_re, a_im, log_dt, b_re, b_im):
    f = jnp.float32
    dt = jnp.exp(log_dt.astype(f))[:, None]
    lr, li = a_re.astype(f), a_im.astype(f)
    mag = jnp.exp(lr * dt)
    ab_re, ab_im = mag * jnp.cos(li * dt), mag * jnp.sin(li * dt)
    den = lr * lr + li * li
    nr = ab_re - 1.0
    co_re = (nr * lr + ab_im * li) / den
    co_im = (ab_im * lr - nr * li) / den
    br, bi = b_re.astype(f), b_im.astype(f)
    bb_re = co_re[..., None] * br - co_im[..., None] * bi
    bb_im = co_re[..., None] * bi + co_im[..., None] * br
    return ab_re, ab_im, bb_re, bb_im


def _complex_affine_combine(e1, e2):
    a1r, a1i, b1r, b1i = e1
    a2r, a2i, b2r, b2i = e2
    return (a2r * a1r - a2i * a1i,
            a2r * a1i + a2i * a1r,
            a2r * b1r - a2i * b1i + b2r,
            a2r * b1i + a2i * b1r + b2i)


def _ssm_branch(u, gate, h0_re, h0_im, a_re, a_im, log_dt, b_re, b_im, c_re, c_im,
                d_skip, w_glu, b_glu, w_ssm_out):
    f = jnp.float32
    bsz, s, _ = u.shape
    uf = u.astype(f)
    ug = uf.reshape(bsz, s, SSM_GROUPS, SSM_GROUP)
    ab_re, ab_im, bb_re, bb_im = _ssm_discretize(a_re, a_im, log_dt, b_re, b_im)
    bu_re = jnp.einsum("gpc,bsgc->bsgp", bb_re, ug)
    bu_im = jnp.einsum("gpc,bsgc->bsgp", bb_im, ug)
    h0r, h0i = h0_re.astype(f), h0_im.astype(f)
    bu_re = bu_re.at[:, 0].add(ab_re * h0r - ab_im * h0i)
    bu_im = bu_im.at[:, 0].add(ab_re * h0i + ab_im * h0r)
    a_re_t = jnp.broadcast_to(ab_re, (1, s) + ab_re.shape)
    a_im_t = jnp.broadcast_to(ab_im, (1, s) + ab_im.shape)
    _, _, xr, xi = lax.associative_scan(_complex_affine_combine,
                                        (a_re_t, a_im_t, bu_re, bu_im), axis=1)
    y = (jnp.einsum("gcp,bsgp->bsgc", c_re.astype(f), xr)
         - jnp.einsum("gcp,bsgp->bsgc", c_im.astype(f), xi)).reshape(bsz, s, D_SSM)
    y = y + d_skip.astype(f) * uf
    g = jax.nn.gelu(y)
    y = g * jax.nn.sigmoid(g @ w_glu.astype(f) + b_glu.astype(f))
    y = (y * jax.nn.silu(gate.astype(f))).astype(u.dtype)
    return y @ w_ssm_out, xr[:, -1].astype(h0_re.dtype), xi[:, -1].astype(h0_re.dtype)


def _finish(x, gate, o_att, g_att, y_ssm, m_att, m_ssm, w_att_out, w_out, ln_g, ln_b):
    bsz, s = x.shape[:2]
    y_att = (o_att.reshape(bsz, s, D_ATT) * jax.nn.silu(g_att)) @ w_att_out
    merged = jax.nn.sigmoid(m_att) * y_att + jax.nn.sigmoid(m_ssm) * y_ssm
    return _layer_norm(DEEPNORM_ALPHA * x + gate * (merged @ w_out), ln_g, ln_b)


def setup_inputs(seed: int = 0) -> dict:
    key = jax.random.key(seed)
    ks = jax.random.split(key, 32)
    f = jnp.float32
    n_pages = PAST_LEN // PAGE_SIZE
    n_phys = (DEC_BATCH * n_pages * 5) // 4
    nrm = lambda k, shape, s: jax.random.normal(k, shape, f) * s

    x_prompt = nrm(ks[0], (BATCH, SEQ, D_MODEL), 1.0)
    x_sample = nrm(ks[1], (DEC_BATCH, DEC_SEQ, D_MODEL), 1.0)
    c_prompt = nrm(ks[2], (BATCH, D_MODEL), 1.0)
    c_sample = nrm(ks[3], (DEC_BATCH, D_MODEL), 1.0)
    cache_k = nrm(ks[4], (DEPTH, n_phys, PAGE_SIZE, ATT_HEADS, HEAD_DIM), 1.0)
    cache_v = nrm(ks[5], (DEPTH, n_phys, PAGE_SIZE, ATT_HEADS, HEAD_DIM), DEEPNORM_BETA)
    state_ssm_re = nrm(ks[6], (DEPTH, DEC_BATCH, SSM_GROUPS, SSM_STATE), 0.3)
    state_ssm_im = nrm(ks[7], (DEPTH, DEC_BATCH, SSM_GROUPS, SSM_STATE), 0.3)
    perm = jax.random.permutation(ks[8], n_phys)
    page_table = perm[:DEC_BATCH * n_pages].reshape(DEC_BATCH, n_pages).astype(jnp.int32)

    w_cond = nrm(ks[9], (DEPTH, D_MODEL, 3 * D_MODEL), 0.5 * D_MODEL ** -0.5)
    b_cond = nrm(ks[10], (DEPTH, 3 * D_MODEL), 0.01)
    col_scale = jnp.concatenate([jnp.ones((2 * D_ATT,), f),
                                 jnp.full((D_ATT,), DEEPNORM_BETA, f),
                                 jnp.ones((D_IN - 3 * D_ATT,), f)])
    w_in = nrm(ks[11], (DEPTH, D_MODEL, D_IN), D_MODEL ** -0.5) * col_scale
    sb_bias = SB_BIAS_INIT + nrm(ks[27], (DEPTH, ATT_HEADS), 0.1)

    ssm_a_re = -0.5 + nrm(ks[12], (DEPTH, SSM_GROUPS, SSM_STATE), 0.01)
    ssm_a_im = (math.pi * jnp.arange(SSM_STATE, dtype=f))[None, None, :] \
        + nrm(ks[13], (DEPTH, SSM_GROUPS, SSM_STATE), 0.01)
    ssm_log_dt = jax.random.uniform(ks[14], (DEPTH, SSM_GROUPS), f,
                                    math.log(DT_MIN), math.log(DT_MAX))
    ssm_b_re = nrm(ks[15], (DEPTH, SSM_GROUPS, SSM_STATE, SSM_GROUP), (2 * SSM_GROUP) ** -0.5)
    ssm_b_im = nrm(ks[16], (DEPTH, SSM_GROUPS, SSM_STATE, SSM_GROUP), (2 * SSM_GROUP) ** -0.5)
    ssm_c_re = nrm(ks[17], (DEPTH, SSM_GROUPS, SSM_GROUP, SSM_STATE), (2 * SSM_STATE) ** -0.5)
    ssm_c_im = nrm(ks[18], (DEPTH, SSM_GROUPS, SSM_GROUP, SSM_STATE), (2 * SSM_STATE) ** -0.5)
    ssm_d = nrm(ks[19], (DEPTH, D_SSM), 1.0)
    w_glu = nrm(ks[20], (DEPTH, D_SSM, D_SSM), D_SSM ** -0.5)
    b_glu = nrm(ks[21], (DEPTH, D_SSM), 0.01)
    w_att_out = nrm(ks[22], (DEPTH, D_ATT, D_MODEL), DEEPNORM_BETA * D_ATT ** -0.5)
    w_ssm_out = nrm(ks[23], (DEPTH, D_SSM, D_MODEL), DEEPNORM_BETA * D_SSM ** -0.5)
    w_out = nrm(ks[24], (DEPTH, D_MODEL, D_MODEL), DEEPNORM_BETA * D_MODEL ** -0.5)
    ln_g = 1.0 + nrm(ks[25], (DEPTH, D_MODEL), 0.02)
    ln_b = nrm(ks[26], (DEPTH, D_MODEL), 0.02)
    return {"x_prompt": x_prompt, "x_sample": x_sample,
            "c_prompt": c_prompt, "c_sample": c_sample,
            "cache_k": cache_k, "cache_v": cache_v,
            "state_ssm_re": state_ssm_re, "state_ssm_im": state_ssm_im,
            "page_table": page_table,
            "w_cond": w_cond, "b_cond": b_cond, "w_in": w_in, "sb_bias": sb_bias,
            "ssm_a_re": ssm_a_re, "ssm_a_im": ssm_a_im, "ssm_log_dt": ssm_log_dt,
            "ssm_b_re": ssm_b_re, "ssm_b_im": ssm_b_im,
            "ssm_c_re": ssm_c_re, "ssm_c_im": ssm_c_im, "ssm_d": ssm_d,
            "w_glu": w_glu, "b_glu": b_glu,
            "w_att_out": w_att_out, "w_ssm_out": w_ssm_out, "w_out": w_out,
            "ln_g": ln_g, "ln_b": ln_b}


def reference(x_prompt, x_sample, c_prompt, c_sample, cache_k, cache_v,
              state_ssm_re, state_ssm_im, page_table,
              w_cond, b_cond, w_in, sb_bias, ssm_a_re, ssm_a_im, ssm_log_dt,
              ssm_b_re, ssm_b_im, ssm_c_re, ssm_c_im, ssm_d, w_glu, b_glu,
              w_att_out, w_ssm_out, w_out, ln_g, ln_b):
    bsz = x_prompt.shape[0]
    dbsz, dseq = x_sample.shape[:2]
    past_len = page_table.shape[1] * cache_k.shape[2]
    q_pos_s = past_len + jnp.arange(dseq, dtype=jnp.int32)
    k_pos_s = jnp.arange(past_len + dseq, dtype=jnp.int32)
    h0_zero = jnp.zeros((bsz, SSM_GROUPS, SSM_STATE), state_ssm_re.dtype)

    xp, xs = x_prompt, x_sample
    kp_l, vp_l, srp_l, sip_l, ks_l, vs_l, srs_l, sis_l = [], [], [], [], [], [], [], []
    for l in range(DEPTH):
        ssm_p = (ssm_a_re[l], ssm_a_im[l], ssm_log_dt[l], ssm_b_re[l], ssm_b_im[l],
                 ssm_c_re[l], ssm_c_im[l], ssm_d[l], w_glu[l], b_glu[l], w_ssm_out[l])
        hp, gate_p = _modulate(xp, c_prompt, w_cond[l], b_cond[l])
        q, k, v, g_att, u, g_ssm, m_att, m_ssm = _project_in(hp, w_in[l])
        o_att = _sb_prompt(q, k, v, sb_bias[l])
        y_ssm, sr, si = _ssm_branch(u, g_ssm, h0_zero, h0_zero, *ssm_p)
        xp = _finish(xp, gate_p, o_att, g_att, y_ssm, m_att, m_ssm,
                     w_att_out[l], w_out[l], ln_g[l], ln_b[l])
        kp_l.append(k); vp_l.append(v); srp_l.append(sr); sip_l.append(si)

        hs, gate_s = _modulate(xs, c_sample, w_cond[l], b_cond[l])
        q, k, v, g_att, u, g_ssm, m_att, m_ssm = _project_in(hs, w_in[l])
        k_past = cache_k[l][page_table].reshape(dbsz, past_len, ATT_HEADS, HEAD_DIM)
        v_past = cache_v[l][page_table].reshape(dbsz, past_len, ATT_HEADS, HEAD_DIM)
        k_all = jnp.concatenate([k_past, k.astype(k_past.dtype)], axis=1)
        v_all = jnp.concatenate([v_past, v.astype(v_past.dtype)], axis=1)
        o_att = _stick_breaking(q, k_all, v_all, q_pos_s, k_pos_s, sb_bias[l]).astype(xs.dtype)
        y_ssm, sr, si = _ssm_branch(u, g_ssm, state_ssm_re[l], state_ssm_im[l], *ssm_p)
        xs = _finish(xs, gate_s, o_att, g_att, y_ssm, m_att, m_ssm,
                     w_att_out[l], w_out[l], ln_g[l], ln_b[l])
        ks_l.append(k); vs_l.append(v); srs_l.append(sr); sis_l.append(si)

    k_prompt, v_prompt = jnp.stack(kp_l), jnp.stack(vp_l)
    ssm_re_prompt, ssm_im_prompt = jnp.stack(srp_l), jnp.stack(sip_l)
    k_sample, v_sample = jnp.stack(ks_l), jnp.stack(vs_l)
    ssm_re_sample, ssm_im_sample = jnp.stack(srs_l), jnp.stack(sis_l)
    return (xp, xs, k_prompt, v_prompt, ssm_re_prompt, ssm_im_prompt,
            k_sample, v_sample, ssm_re_sample, ssm_im_sample)
```

```python
import functools
import math

import jax
import jax.numpy as jnp
from jax import lax
from jax.experimental import pallas as pl
from jax.experimental.pallas import tpu as pltpu

F32 = jnp.float32
BF16 = jnp.bfloat16
LN_EPS = 1e-5
GROUPS_PER_BLOCK = 8
VMEM_LIMIT = 56 * 1024 * 1024


def _dot(a, b):
    return jnp.dot(a, b, preferred_element_type=F32)


def _split(a):
    hi = a.astype(BF16)
    lo = (a - hi.astype(F32)).astype(BF16)
    return hi, lo


def _dot3(a, b):
    ah, al = _split(a)
    bh, bl = _split(b)
    return _dot(ah, bh) + (_dot(ah, bl) + _dot(al, bh))


def _softplus(z):
    return jnp.maximum(z, 0.0) + jnp.log1p(jnp.exp(-jnp.abs(z)))


def _params(*sem):
    return pltpu.CompilerParams(dimension_semantics=sem, vmem_limit_bytes=VMEM_LIMIT)


def _mod_kernel(c_ref, w_ref, b_ref, o_ref):
    o_ref[...] = _dot3(c_ref[...], w_ref[...]) + b_ref[...]


def _modulation(c, w_cond, b_cond):
    rows, d = c.shape
    n = w_cond.shape[1]
    tn = 768 if n % 768 == 0 else n
    return pl.pallas_call(
        _mod_kernel,
        grid=(n // tn,),
        in_specs=[pl.BlockSpec((rows, d), lambda j: (0, 0)),
                  pl.BlockSpec((d, tn), lambda j: (0, j)),
                  pl.BlockSpec((1, tn), lambda j: (0, j))],
        out_specs=pl.BlockSpec((rows, tn), lambda j: (0, j)),
        out_shape=jax.ShapeDtypeStruct((rows, n), F32),
        compiler_params=_params("arbitrary"),
        name="mod",
    )(c, w_cond, b_cond.reshape(1, n))


def _inproj_kernel(x_ref, scale_ref, shift_ref, w_ref, k_ref, v_ref, z_ref, h_ref, *, precise):
    j = pl.program_id(2)

    @pl.when(j == 0)
    def _():
        h = x_ref[0] * (1.0 + scale_ref[0]) + shift_ref[0]
        h_ref[...] = h.astype(h_ref.dtype)

    if precise:
        acc = _dot3(h_ref[...], w_ref[...])
    else:
        acc = _dot(h_ref[...], w_ref[...])
    z_ref[0] = acc.astype(z_ref.dtype)

    @pl.when(j == 1)
    def _():
        k_ref[0] = acc

    @pl.when(j == 2)
    def _():
        v_ref[0] = acc


def _inproj(x, scale, shift, w_in, *, tm, precise, z_dtype):
    b, s, d = x.shape
    n = w_in.shape[1]
    assert n == 8 * d and s % tm == 0
    per_row = scale.shape[1] != 1
    mod_spec = (pl.BlockSpec((1, tm, d), lambda bi, m, j: (bi, m, 0)) if per_row
                else pl.BlockSpec((1, 1, d), lambda bi, m, j: (bi, 0, 0)))
    row_spec = pl.BlockSpec((1, tm, d), lambda bi, m, j: (bi, m, 0))
    return pl.pallas_call(
        functools.partial(_inproj_kernel, precise=precise),
        grid=(b, s // tm, n // d),
        in_specs=[row_spec, mod_spec, mod_spec,
                  pl.BlockSpec((d, d), lambda bi, m, j: (0, j))],
        out_specs=[row_spec, row_spec,
                   pl.BlockSpec((1, tm, d), lambda bi, m, j: (bi, m, j))],
        out_shape=[jax.ShapeDtypeStruct((b, s, d), F32),
                   jax.ShapeDtypeStruct((b, s, d), F32),
                   jax.ShapeDtypeStruct((b, s, n), z_dtype)],
        scratch_shapes=[pltpu.VMEM((tm, d), F32 if precise else BF16)],
        compiler_params=_params("parallel", "parallel", "arbitrary"),
        name="inproj",
    )(x, scale, shift, w_in)


def _attn_kernel(bias_ref, q_ref, k_ref, v_ref, tri_ref, o_ref, *, tq, hd, heads_per_block):
    hp = pl.program_id(1)
    i = pl.program_id(2)
    q = q_ref[0] * jnp.asarray(hd ** -0.5, q_ref.dtype)
    tri = tri_ref[...]
    row = lax.broadcasted_iota(jnp.int32, (tq, tq), 0)
    col = lax.broadcasted_iota(jnp.int32, (tq, tq), 1)
    outs = []
    for hh in range(heads_per_block):
        lanes = slice(hh * hd, (hh + 1) * hd)
        qh = q[:, lanes]
        bias = bias_ref[hp * heads_per_block + hh]

        def body(jj, carry, lanes=lanes, qh=qh, bias=bias):
            o_acc, run = carry
            start = pl.multiple_of((i - jj) * tq, tq)
            ks = k_ref[0, pl.ds(start, tq), lanes]
            vs = v_ref[0, pl.ds(start, tq), lanes]
            z = lax.dot_general(qh, ks, (((1,), (1,)), ((), ())), preferred_element_type=F32) + bias
            valid = col < row + jj * tq
            sp = jnp.where(valid, _softplus(z), 0.0)
            hi, lo = _split(sp)
            later = _dot(hi, tri) + _dot(lo, tri)
            w = jnp.where(valid, jnp.exp(z - sp - later - run), 0.0)
            o_acc = o_acc + _dot(w.astype(BF16), vs)
            run = run + jnp.sum(sp, axis=-1, keepdims=True)
            return o_acc, run

        o_acc, _ = lax.fori_loop(0, i + 1, body,
                                 (jnp.zeros((tq, hd), F32), jnp.zeros((tq, 1), F32)))
        outs.append(o_acc)
    o_ref[0] = jnp.concatenate(outs, axis=-1).astype(o_ref.dtype)


def _prompt_attention(z, sb_bias, *, d, hd, tq):
    b, s, _ = z.shape
    lane_block = 128
    heads_per_block = lane_block // hd
    nhb = d // lane_block
    r = lax.broadcasted_iota(jnp.int32, (tq, tq), 0)
    c = lax.broadcasted_iota(jnp.int32, (tq, tq), 1)
    tri = (r > c).astype(BF16)
    return pl.pallas_call(
        functools.partial(_attn_kernel, tq=tq, hd=hd, heads_per_block=heads_per_block),
        grid=(b, nhb, s // tq),
        in_specs=[pl.BlockSpec(memory_space=pltpu.SMEM),
                  pl.BlockSpec((1, tq, lane_block), lambda bi, hp, i: (bi, i, hp)),
                  pl.BlockSpec((1, s, lane_block), lambda bi, hp, i: (bi, 0, nhb + hp)),
                  pl.BlockSpec((1, s, lane_block), lambda bi, hp, i: (bi, 0, 2 * nhb + hp)),
                  pl.BlockSpec((tq, tq), lambda bi, hp, i: (0, 0))],
        out_specs=pl.BlockSpec((1, tq, lane_block), lambda bi, hp, i: (bi, i, hp)),
        out_shape=jax.ShapeDtypeStruct((b, s, d), BF16),
        compiler_params=_params("parallel", "parallel", "arbitrary"),
        name="attn",
    )(sb_bias.astype(F32), z, z, z, tri)


def _decode_kernel(pt_ref, q_ref, bias_ref, k_ref, v_ref, ind_ref, indt_ref, upper_ref, o_ref,
                   acc_ref, run_ref, *, hd):
    p = pl.program_id(1)

    @pl.when(p == 0)
    def _():
        acc_ref[...] = jnp.zeros_like(acc_ref)
        run_ref[...] = jnp.zeros_like(run_ref)

    ind = ind_ref[...]
    prod = k_ref[0] * (q_ref[0] * (hd ** -0.5))
    ph, plo = _split(prod)
    z = _dot(ph, ind) + _dot(plo, ind) + bias_ref[...]
    sp = _softplus(z)
    sh, sl = _split(sp)
    upper = upper_ref[...]
    later = _dot(upper, sh) + _dot(upper, sl)
    w = jnp.exp(z - sp - later - run_ref[...])
    run_ref[...] += jnp.sum(sp, axis=0, keepdims=True)
    wh, wl = _split(w)
    indt = indt_ref[...]
    wexp = _dot(wh, indt) + _dot(wl, indt)
    contrib = wexp * v_ref[0]
    page, dd = contrib.shape
    acc_ref[...] += contrib.reshape(page // 8, 8, dd).sum(axis=0)

    @pl.when(p == pl.num_programs(1) - 1)
    def _():
        o_ref[0] = jnp.sum(acc_ref[...], axis=0, keepdims=True)


def _decode_attention(q, cache_k, cache_v, page_table, sb_bias, *, hd):
    b, _, d = q.shape
    n_phys, page, _ = cache_k.shape
    n_pages = page_table.shape[1]
    h = 128
    assert d // hd <= h
    lane_head = lax.broadcasted_iota(jnp.int32, (d, h), 0) // hd
    ind = (lane_head == lax.broadcasted_iota(jnp.int32, (d, h), 1)).astype(BF16)
    bias = jnp.zeros((1, h), F32).at[0, :d // hd].set(sb_bias.astype(F32))
    r = lax.broadcasted_iota(jnp.int32, (page, page), 0)
    c = lax.broadcasted_iota(jnp.int32, (page, page), 1)
    upper = (c > r).astype(BF16)

    def page_map(bi, p, pt):
        return (pt[bi, n_pages - 1 - p], 0, 0)

    const = lambda bi, p, pt: (0, 0)
    grid_spec = pltpu.PrefetchScalarGridSpec(
        num_scalar_prefetch=1,
        grid=(b, n_pages),
        in_specs=[pl.BlockSpec((1, 1, d), lambda bi, p, pt: (bi, 0, 0)),
                  pl.BlockSpec((1, h), const),
                  pl.BlockSpec((1, page, d), page_map),
                  pl.BlockSpec((1, page, d), page_map),
                  pl.BlockSpec((d, h), const),
                  pl.BlockSpec((h, d), const),
                  pl.BlockSpec((page, page), const)],
        out_specs=pl.BlockSpec((1, 1, d), lambda bi, p, pt: (bi, 0, 0)),
        scratch_shapes=[pltpu.VMEM((8, d), F32), pltpu.VMEM((1, h), F32)],
    )
    return pl.pallas_call(
        functools.partial(_decode_kernel, hd=hd),
        grid_spec=grid_spec,
        out_shape=jax.ShapeDtypeStruct((b, 1, d), F32),
        compiler_params=_params("parallel", "arbitrary"),
        name="decode",
    )(page_table, q, bias, cache_k, cache_v, ind, ind.T, upper)


def _ssmprep_kernel(ar_ref, ai_ref, ldt_ref, br_ref, bi_ref, abr_ref, abi_ref, bbr_ref, bbi_ref):
    dt = jnp.exp(ldt_ref[...])
    lr, li = ar_ref[...], ai_ref[...]
    mag = jnp.exp(lr * dt)
    ab_re, ab_im = mag * jnp.cos(li * dt), mag * jnp.sin(li * dt)
    den = lr * lr + li * li
    nr = ab_re - 1.0
    co_re = (nr * lr + ab_im * li) / den
    co_im = (ab_im * lr - nr * li) / den
    abr_ref[...] = ab_re
    abi_ref[...] = ab_im
    br, bi = br_ref[...], bi_ref[...]
    bbr_ref[...] = co_re * br - co_im * bi
    bbi_ref[...] = co_re * bi + co_im * br


def _ssm_discretize(a_re, a_im, log_dt, b_re, b_im):
    g, p = a_re.shape
    c = b_re.shape[-1]
    rep = lambda a: jnp.repeat(a, c, axis=0)
    rows = lambda b: b.transpose(0, 2, 1).reshape(g * c, p)
    out = jax.ShapeDtypeStruct((g * c, p), F32)
    ab_re, ab_im, bb_re, bb_im = pl.pallas_call(
        _ssmprep_kernel, out_shape=[out] * 4, name="ssmprep",
    )(rep(a_re), rep(a_im), rep(log_dt.reshape(g, 1)), rows(b_re), rows(b_im))
    return ab_re[::c], ab_im[::c], bb_re.reshape(g, c, p), bb_im.reshape(g, c, p)


def _block_diag(w, gpb):
    g, r, k = w.shape
    w = w.reshape(g // gpb, gpb, r, k)
    eye = jnp.eye(gpb, dtype=w.dtype)
    return jnp.einsum("bgrk,gh->bgrhk", w, eye).reshape(g // gpb, gpb * r, gpb * k)


def _ssm_kernel(u_ref, h0r_ref, h0i_ref, bmat_ref, ar_ref, ai_ref, cmat_ref, d_ref,
                y_ref, xr_ref, xi_ref, bu_ref, sr_ref, si_ref, *, tt, nb, ns, precise):
    t = pl.program_id(1)

    @pl.when(t == 0)
    def _():
        sr_ref[...] = h0r_ref[...]
        si_ref[...] = h0i_ref[...]

    u = u_ref[...].reshape(tt * nb, u_ref.shape[-1])
    if precise:
        bu_ref[...] = _dot3(u, bmat_ref[0])
    else:
        bu_ref[...] = _dot(u.astype(BF16), bmat_ref[0].astype(BF16))
    ar = jnp.broadcast_to(ar_ref[0], (nb, ns))
    ai = jnp.broadcast_to(ai_ref[0], (nb, ns))

    def step(k, carry):
        xr, xi = carry
        rows = pl.ds(pl.multiple_of(k * nb, nb), nb)
        nr = ar * xr - ai * xi + bu_ref[rows, :ns]
        ni = ar * xi + ai * xr + bu_ref[rows, ns:]
        bu_ref[rows, :ns] = nr
        bu_ref[rows, ns:] = ni
        return nr, ni

    xr, xi = lax.fori_loop(0, tt, step, (sr_ref[...], si_ref[...]))
    sr_ref[...] = xr
    si_ref[...] = xi
    if precise:
        y = _dot3(bu_ref[...], cmat_ref[0])
    else:
        y = _dot(bu_ref[...].astype(BF16), cmat_ref[0].astype(BF16))
    y = y + d_ref[...] * u
    y_ref[...] = y.reshape(y_ref.shape).astype(y_ref.dtype)

    @pl.when(t == pl.num_programs(1) - 1)
    def _():
        xr_ref[...] = xr
        xi_ref[...] = xi


def _ssm_scan(u, h0_re, h0_im, bmat, ab_re, ab_im, cmat, d_skip, *, tt, precise):
    s, nb, d = u.shape
    nblk, ch, ns2 = bmat.shape
    ns = ns2 // 2
    assert s % tt == 0 and d == nblk * ch
    grid = (nblk, s // tt)
    state_spec = pl.BlockSpec((nb, ns), lambda g, t: (0, g))
    return pl.pallas_call(
        functools.partial(_ssm_kernel, tt=tt, nb=nb, ns=ns, precise=precise),
        grid=grid,
        in_specs=[pl.BlockSpec((tt, nb, ch), lambda g, t: (t, 0, g)),
                  state_spec, state_spec,
                  pl.BlockSpec((1, ch, ns2), lambda g, t: (g, 0, 0)),
                  pl.BlockSpec((1, 1, ns), lambda g, t: (g, 0, 0)),
                  pl.BlockSpec((1, 1, ns), lambda g, t: (g, 0, 0)),
                  pl.BlockSpec((1, ns2, ch), lambda g, t: (g, 0, 0)),
                  pl.BlockSpec((1, ch), lambda g, t: (0, g))],
        out_specs=[pl.BlockSpec((tt, nb, ch), lambda g, t: (t, 0, g)), state_spec, state_spec],
        out_shape=[jax.ShapeDtypeStruct((s, nb, d), F32),
                   jax.ShapeDtypeStruct(h0_re.shape, F32),
                   jax.ShapeDtypeStruct(h0_re.shape, F32)],
        scratch_shapes=[pltpu.VMEM((tt * nb, ns2), F32),
                        pltpu.VMEM((nb, ns), F32), pltpu.VMEM((nb, ns), F32)],
        compiler_params=_params("parallel", "arbitrary"),
        name="ssm",
    )(u, h0_re, h0_im, bmat, ab_re, ab_im, cmat, d_skip)


def _post_kernel(x_ref, gate_ref, y_ref, o_ref, gatt_ref, gssm_ref, matt_ref, mssm_ref,
                 wglu_ref, bglu_ref, watt_ref, wssm_ref, wout_ref, lng_ref, lnb_ref, out_ref, *, alpha):
    g = jax.nn.gelu(y_ref[0].astype(F32))
    glu = g * jax.nn.sigmoid(_dot(g.astype(BF16), wglu_ref[...]) + bglu_ref[...])
    ys = glu * jax.nn.silu(gssm_ref[0].astype(F32))
    y_ssm = _dot(ys.astype(BF16), wssm_ref[...])
    ya = o_ref[0].astype(F32) * jax.nn.silu(gatt_ref[0].astype(F32))
    y_att = _dot(ya.astype(BF16), watt_ref[...])
    merged = (jax.nn.sigmoid(matt_ref[0].astype(F32)) * y_att
              + jax.nn.sigmoid(mssm_ref[0].astype(F32)) * y_ssm)
    r = alpha * x_ref[0] + gate_ref[0] * _dot(merged.astype(BF16), wout_ref[...])
    mu = jnp.mean(r, axis=-1, keepdims=True)
    cen = r - mu
    var = jnp.mean(cen * cen, axis=-1, keepdims=True)
    out_ref[0] = cen * lax.rsqrt(var + LN_EPS) * lng_ref[...] + lnb_ref[...]


def _post(x, gate, y, o_att, z, w_glu, b_glu, w_att_out, w_ssm_out, w_out, ln_g, ln_b, *, tm, alpha):
    b, s, d = x.shape
    per_row = gate.shape[1] != 1
    row = lambda col: pl.BlockSpec((1, tm, d), lambda bi, m, col=col: (bi, m, col))
    gate_spec = row(0) if per_row else pl.BlockSpec((1, 1, d), lambda bi, m: (bi, 0, 0))
    mat = pl.BlockSpec((d, d), lambda bi, m: (0, 0))
    vec = pl.BlockSpec((1, d), lambda bi, m: (0, 0))
    return pl.pallas_call(
        functools.partial(_post_kernel, alpha=alpha),
        grid=(b, s // tm),
        in_specs=[row(0), gate_spec, row(0), row(0), row(3), row(5), row(6), row(7),
                  mat, vec, mat, mat, mat, vec, vec],
        out_specs=row(0),
        out_shape=jax.ShapeDtypeStruct((b, s, d), F32),
        compiler_params=_params("parallel", "parallel"),
        name="post",
    )(x, gate, y, o_att, z, z, z, z,
      w_glu.astype(BF16), b_glu.reshape(1, d), w_att_out.astype(BF16), w_ssm_out.astype(BF16),
      w_out.astype(BF16), ln_g.reshape(1, d), ln_b.reshape(1, d))


def _pick(n, target):
    t = min(n, target)
    while n % t:
        t //= 2
    return t


def kernel(x_prompt, x_sample, c_prompt, c_sample, cache_k, cache_v, state_ssm_re, state_ssm_im, page_table, w_cond, b_cond, w_in, sb_bias, ssm_a_re, ssm_a_im, ssm_log_dt, ssm_b_re, ssm_b_im, ssm_c_re, ssm_c_im, ssm_d, w_glu, b_glu, w_att_out, w_ssm_out, w_out, ln_g, ln_b):
    depth = w_in.shape[0]
    assert depth == 1, "single-layer trunk"
    bsz, seq, d = x_prompt.shape
    dbsz, dseq, _ = x_sample.shape
    assert dseq == 1
    n_heads = sb_bias.shape[-1]
    hd = cache_k.shape[-1]
    page = cache_k.shape[2]
    n_groups, n_state = ssm_a_re.shape[1:]
    alpha = (2.0 * depth) ** 0.25
    l = 0

    mod = _modulation(jnp.concatenate([c_prompt, c_sample], axis=0), w_cond[l], b_cond[l])
    shift, scale, gate = mod[:, :d], mod[:, d:2 * d], mod[:, 2 * d:]
    row3 = lambda a, lo, hi, shape: a[lo:hi].reshape(shape)
    p_shape, s_shape = (bsz, 1, d), (1, dbsz, d)

    ab_re, ab_im, bb_re, bb_im = _ssm_discretize(ssm_a_re[l], ssm_a_im[l], ssm_log_dt[l],
                                                 ssm_b_re[l], ssm_b_im[l])
    gpb = GROUPS_PER_BLOCK
    nblk = n_groups // gpb
    bmat = jnp.concatenate([_block_diag(bb_re, gpb), _block_diag(bb_im, gpb)], axis=-1)
    cmat = jnp.concatenate([_block_diag(ssm_c_re[l].transpose(0, 2, 1), gpb),
                            _block_diag(-ssm_c_im[l].transpose(0, 2, 1), gpb)], axis=1)
    a_blk = lambda a: a.reshape(nblk, 1, gpb * n_state)
    d_skip = ssm_d[l].reshape(1, d)

    k_p, v_p, z_p = _inproj(x_prompt, row3(scale, 0, bsz, p_shape), row3(shift, 0, bsz, p_shape),
                            w_in[l].astype(BF16), tm=_pick(seq, 512), precise=False, z_dtype=BF16)
    o_p = _prompt_attention(z_p, sb_bias[l], d=d, hd=hd, tq=_pick(seq, 256))
    u_p = z_p[:, :, 4 * d:5 * d].astype(F32).transpose(1, 0, 2)
    zeros = jnp.zeros((bsz, n_groups * n_state), F32)
    y_p, sr_p, si_p = _ssm_scan(u_p, zeros, zeros, bmat, a_blk(ab_re), a_blk(ab_im), cmat, d_skip,
                                tt=_pick(seq, 256), precise=False)
    out_p = _post(x_prompt, row3(gate, 0, bsz, p_shape), y_p.transpose(1, 0, 2), o_p, z_p,
                  w_glu[l], b_glu[l], w_att_out[l], w_ssm_out[l], w_out[l], ln_g[l], ln_b[l],
                  tm=_pick(seq, 512), alpha=alpha)

    x_s = x_sample.reshape(1, dbsz, d)
    k_s, v_s, z_s = _inproj(x_s, row3(scale, bsz, bsz + dbsz, s_shape), row3(shift, bsz, bsz + dbsz, s_shape),
                            w_in[l], tm=dbsz, precise=True, z_dtype=F32)
    o_s = _decode_attention(z_s[0, :, :d].reshape(dbsz, 1, d),
                            cache_k[l].reshape(-1, page, d), cache_v[l].reshape(-1, page, d),
                            page_table, sb_bias[l], hd=hd)
    y_s, sr_s, si_s = _ssm_scan(z_s[:, :, 4 * d:5 * d], state_ssm_re[l].reshape(dbsz, -1),
                                state_ssm_im[l].reshape(dbsz, -1), bmat, a_blk(ab_re), a_blk(ab_im),
                                cmat, d_skip, tt=1, precise=True)
    out_s = _post(x_s, row3(gate, bsz, bsz + dbsz, s_shape), y_s, o_s.reshape(1, dbsz, d), z_s,
                  w_glu[l], b_glu[l], w_att_out[l], w_ssm_out[l], w_out[l], ln_g[l], ln_b[l],
                  tm=dbsz, alpha=alpha)

    heads = lambda t, b, s: t.reshape(1, b, s, n_heads, hd)
    state = lambda t, b: t.reshape(1, b, n_groups, n_state)
    return (out_p, out_s.reshape(dbsz, 1, d),
            heads(k_p, bsz, seq), heads(v_p, bsz, seq), state(sr_p, bsz), state(si_p, bsz),
            heads(k_s, dbsz, 1), heads(v_s, dbsz, 1), state(sr_s, dbsz), state(si_s, dbsz))
```

```python
import functools
import math

import jax
import jax.numpy as jnp
from jax import lax
from jax.experimental import pallas as pl
from jax.experimental.pallas import tpu as pltpu

F32 = jnp.float32
BF16 = jnp.bfloat16
LN_EPS = 1e-5
LOG2E = math.log2(math.e)
LANES = 128
GROUPS_PER_BLOCK = 8
VMEM_LIMIT = 56 * 1024 * 1024
NT_DIMS = (((1,), (1,)), ((), ()))


def _dot(a, b, dims=None):
    if dims is None:
        return jnp.dot(a, b, preferred_element_type=F32)
    return lax.dot_general(a, b, dims, preferred_element_type=F32)


def _split(a):
    hi = a.astype(BF16)
    lo = (a - hi.astype(F32)).astype(BF16)
    return hi, lo


def _dot3(a, b, dims=None):
    ah, al = _split(a)
    bh, bl = _split(b)
    return _dot(ah, bh, dims) + (_dot(ah, bl, dims) + _dot(al, bh, dims))


def _dot_hilo(a, b_bf16):
    hi, lo = _split(a)
    return _dot(hi, b_bf16) + _dot(lo, b_bf16)


def _softplus2(y):
    neg_abs = lax.bitcast_convert_type(
        lax.bitcast_convert_type(y, jnp.uint32) | jnp.uint32(0x80000000), F32)
    return jnp.maximum(y, 0.0) + jnp.log2(1.0 + jnp.exp2(neg_abs))


def _params(*sem):
    return pltpu.CompilerParams(dimension_semantics=sem, vmem_limit_bytes=VMEM_LIMIT)


def _mod_kernel(c_ref, w_ref, b_ref, o_ref):
    o_ref[...] = _dot3(c_ref[...], w_ref[...]) + b_ref[...]


def _modulation(c, w_cond, b_cond):
    rows, d = c.shape
    n = w_cond.shape[1]
    tn = 768 if n % 768 == 0 else n
    return pl.pallas_call(
        _mod_kernel,
        grid=(n // tn,),
        in_specs=[pl.BlockSpec((rows, d), lambda j: (0, 0)),
                  pl.BlockSpec((d, tn), lambda j: (0, j)),
                  pl.BlockSpec((1, tn), lambda j: (0, j))],
        out_specs=pl.BlockSpec((rows, tn), lambda j: (0, j)),
        out_shape=jax.ShapeDtypeStruct((rows, n), F32),
        compiler_params=_params("arbitrary"),
        name="mod",
    )(c, w_cond, b_cond.reshape(1, n))


def _inproj_kernel(x_ref, scale_ref, shift_ref, w_ref, t_ref, z_ref, h_ref, *, precise, nt):
    j = pl.program_id(2)
    dot = _dot3 if precise else _dot

    @pl.when(j == 0)
    def _():
        h = x_ref[0] * (1.0 + scale_ref[0]) + shift_ref[0]
        h_ref[...] = h.astype(h_ref.dtype)

    @pl.when(j < nt)
    def _():
        t_ref[0, 0] = dot(w_ref[...], h_ref[...], NT_DIMS)

    @pl.when(j >= nt)
    def _():
        z_ref[0] = dot(h_ref[...], w_ref[...]).astype(z_ref.dtype)


def _inproj(x, scale, shift, w_groups, *, nt, tm, precise, z_dtype):
    b, s, d = x.shape
    ng = w_groups.shape[1] // d
    assert s % tm == 0
    per_row = scale.shape[1] != 1
    mod_spec = (pl.BlockSpec((1, tm, d), lambda bi, m, j: (bi, m, 0)) if per_row
                else pl.BlockSpec((1, 1, d), lambda bi, m, j: (bi, 0, 0)))
    return pl.pallas_call(
        functools.partial(_inproj_kernel, precise=precise, nt=nt),
        grid=(b, s // tm, ng),
        in_specs=[pl.BlockSpec((1, tm, d), lambda bi, m, j: (bi, m, 0)), mod_spec, mod_spec,
                  pl.BlockSpec((d, d), lambda bi, m, j: (0, j))],
        out_specs=[pl.BlockSpec((1, 1, d, tm), lambda bi, m, j: (jnp.minimum(j, nt - 1), bi, 0, m)),
                   pl.BlockSpec((1, tm, d), lambda bi, m, j: (bi, m, jnp.maximum(j - nt, 0)))],
        out_shape=[jax.ShapeDtypeStruct((nt, b, d, s), F32),
                   jax.ShapeDtypeStruct((b, s, (ng - nt) * d), z_dtype)],
        scratch_shapes=[pltpu.VMEM((tm, d), F32 if precise else BF16)],
        compiler_params=_params("parallel", "parallel", "arbitrary"),
        name="inproj",
    )(x, scale, shift, w_groups)


def _attn_kernel(bias_ref, q_ref, kt_ref, vt_ref, tri_ref, o_ref, y_ref, incl_ref, acc_ref, run_ref,
                 *, tq, hd, nh):
    g = pl.program_id(1)
    i = pl.program_id(2)
    q = q_ref[0]
    tri = tri_ref[...]
    row = lax.broadcasted_iota(jnp.int32, (tq, tq), 0)
    col = lax.broadcasted_iota(jnp.int32, (tq, tq), 1)
    below_diag = col < row
    logit_scale = hd ** -0.5 * LOG2E

    def block_start(jj):
        return pl.multiple_of((i - jj) * tq, tq)

    def logits(jj, slot, diagonal=False):
        for hh in range(nh):
            rows = slice(hh * hd, (hh + 1) * hd)
            kt = kt_ref[0, 0, rows, pl.ds(block_start(jj), tq)].astype(BF16)
            y = _dot(q[:, rows], kt) * logit_scale + bias_ref[g * nh + hh] * LOG2E
            if diagonal:
                y = jnp.where(below_diag, y, -1e30)
            sp = _softplus2(y)
            hi = lax.bitcast_convert_type(
                lax.bitcast_convert_type(sp, jnp.uint32) & jnp.uint32(0xFFFF0000), F32)
            parts = jnp.concatenate([hi.astype(BF16), (sp - hi).astype(BF16)], axis=1)
            y_ref[slot, hh] = y
            incl_ref[slot, hh] = _dot(parts, tri)

    def weights(jj, slot):
        for hh in range(nh):
            rows = slice(hh * hd, (hh + 1) * hd)
            vt = vt_ref[0, 0, rows, pl.ds(block_start(jj), tq)].astype(BF16)
            incl = incl_ref[slot, hh]
            run = run_ref[hh]
            w = jnp.exp2(y_ref[slot, hh] - incl - run)
            acc_ref[hh] += _dot(w.astype(BF16), vt, NT_DIMS)
            run_ref[hh] = run + incl[:, :1]

    acc_ref[...] = jnp.zeros_like(acc_ref)
    run_ref[...] = jnp.zeros_like(run_ref)
    logits(0, 0, diagonal=True)

    @pl.loop(0, i // 2)
    def _(m):
        logits(2 * m + 1, 1)
        weights(2 * m, 0)
        logits(2 * m + 2, 0)
        weights(2 * m + 1, 1)

    @pl.when(i % 2 == 1)
    def _():
        logits(i, 1)
        weights(i - 1, 0)
        weights(i, 1)

    @pl.when(i % 2 == 0)
    def _():
        weights(i, 0)

    o_ref[0] = jnp.concatenate([acc_ref[hh] for hh in range(nh)], axis=-1).astype(o_ref.dtype)


def _prompt_attention(z, kvt, sb_bias, *, d, hd, tq, nh):
    b, s, _ = z.shape
    width = nh * hd
    r = lax.broadcasted_iota(jnp.int32, (2 * tq, tq), 0) % tq
    c = lax.broadcasted_iota(jnp.int32, (2 * tq, tq), 1)
    tri = (r >= c).astype(BF16)
    return pl.pallas_call(
        functools.partial(_attn_kernel, tq=tq, hd=hd, nh=nh),
        grid=(b, d // width, s // tq),
        in_specs=[pl.BlockSpec(memory_space=pltpu.SMEM),
                  pl.BlockSpec((1, tq, width), lambda bi, g, i: (bi, i, g)),
                  pl.BlockSpec((1, 1, width, s), lambda bi, g, i: (0, bi, g, 0)),
                  pl.BlockSpec((1, 1, width, s), lambda bi, g, i: (1, bi, g, 0)),
                  pl.BlockSpec((2 * tq, tq), lambda bi, g, i: (0, 0))],
        out_specs=pl.BlockSpec((1, tq, width), lambda bi, g, i: (bi, i, g)),
        out_shape=jax.ShapeDtypeStruct((b, s, d), BF16),
        scratch_shapes=[pltpu.VMEM((2, nh, tq, tq), F32), pltpu.VMEM((2, nh, tq, tq), F32),
                        pltpu.VMEM((nh, tq, hd), F32), pltpu.VMEM((nh, tq, 1), F32)],
        compiler_params=_params("parallel", "parallel", "arbitrary"),
        name="attn",
    )(sb_bias.astype(F32), z, kvt, kvt, tri)


def _decode_kernel(pt_ref, qt_ref, bias_ref, tri_ref, *refs, hd, npg):
    k_refs, v_refs = refs[:npg], refs[npg:2 * npg]
    o_ref, qb_ref, acc_ref, run_ref = refs[2 * npg:]
    bi = pl.program_id(0)
    p = pl.program_id(1)
    nh, _, page = acc_ref.shape

    @pl.when(p == 0)
    def _():
        nb = qt_ref.shape[1]
        onehot = (lax.broadcasted_iota(jnp.int32, (nb, page), 0) == bi).astype(BF16)
        qb = _dot_hilo(qt_ref[...], onehot) * (hd ** -0.5 * LOG2E)
        qb_ref[...] = qb.reshape(qb_ref.shape)
        acc_ref[...] = jnp.zeros_like(acc_ref)
        run_ref[...] = jnp.zeros_like(run_ref)

    qb = qb_ref[...]
    bias2 = bias_ref[...] * LOG2E
    tri = tri_ref[...]
    run = run_ref[...]
    weights = []
    for kr in k_refs:
        y = jnp.sum(kr[0] * qb, axis=1) + bias2
        sp = _softplus2(y)
        ext = _dot_hilo(sp, tri)
        weights.append(jnp.exp2(y - sp - ext[:, :page] - run))
        run = run + ext[:, page:]
    run_ref[...] = run
    for h in range(nh):
        part = v_refs[0][0, h] * weights[0][h:h + 1, :]
        for i in range(1, npg):
            part = part + v_refs[i][0, h] * weights[i][h:h + 1, :]
        acc_ref[h] += part

    @pl.when(p == pl.num_programs(1) - 1)
    def _():
        o_ref[0] = jnp.sum(acc_ref[...], axis=-1)


def _decode_attention(qt, cache_kt, cache_vt, page_table, sb_bias, *, npg):
    d, b = qt.shape
    _, nh, hd, page = cache_kt.shape
    n_pages = page_table.shape[1]
    assert n_pages % npg == 0 and page == LANES
    r = lax.broadcasted_iota(jnp.int32, (page, 2 * page), 0)
    c = lax.broadcasted_iota(jnp.int32, (page, 2 * page), 1)
    tri = jnp.logical_or(r > c, c >= page).astype(BF16)
    bias = jnp.broadcast_to(sb_bias.astype(F32)[:, None], (nh, page))

    def page_spec(i):
        return pl.BlockSpec((1, nh, hd, page),
                            lambda bi, p, pt: (pt[bi, n_pages - 1 - (p * npg + i)], 0, 0, 0))

    const = lambda bi, p, pt: (0, 0)
    grid_spec = pltpu.PrefetchScalarGridSpec(
        num_scalar_prefetch=1,
        grid=(b, n_pages // npg),
        in_specs=[pl.BlockSpec((d, b), const), pl.BlockSpec((nh, page), const),
                  pl.BlockSpec((page, 2 * page), const)]
                 + [page_spec(i) for i in range(npg)] * 2,
        out_specs=pl.BlockSpec((1, nh, hd), lambda bi, p, pt: (bi, 0, 0)),
        scratch_shapes=[pltpu.VMEM((nh, hd, page), F32), pltpu.VMEM((nh, hd, page), F32),
                        pltpu.VMEM((nh, page), F32)],
    )
    return pl.pallas_call(
        functools.partial(_decode_kernel, hd=hd, npg=npg),
        grid_spec=grid_spec,
        out_shape=jax.ShapeDtypeStruct((b, nh, hd), F32),
        compiler_params=_params("parallel", "arbitrary"),
        name="decode",
    )(page_table, qt, bias, tri, *([cache_kt] * npg), *([cache_vt] * npg))


def _ssmprep_kernel(ar_ref, ai_ref, ldt_ref, br_ref, bi_ref, abr_ref, abi_ref, bbr_ref, bbi_ref):
    dt = jnp.exp(ldt_ref[...])
    lr, li = ar_ref[...], ai_ref[...]
    mag = jnp.exp(lr * dt)
    ab_re, ab_im = mag * jnp.cos(li * dt), mag * jnp.sin(li * dt)
    den = lr * lr + li * li
    nr = ab_re - 1.0
    co_re = (nr * lr + ab_im * li) / den
    co_im = (ab_im * lr - nr * li) / den
    abr_ref[...] = ab_re
    abi_ref[...] = ab_im
    br, bi = br_ref[...], bi_ref[...]
    bbr_ref[...] = co_re * br - co_im * bi
    bbi_ref[...] = co_re * bi + co_im * br


def _ssm_discretize(a_re, a_im, log_dt, b_re, b_im):
    g, p = a_re.shape
    c = b_re.shape[-1]
    rep = lambda a: jnp.repeat(a, c, axis=0)
    rows = lambda b: b.transpose(0, 2, 1).reshape(g * c, p)
    out = jax.ShapeDtypeStruct((g * c, p), F32)
    ab_re, ab_im, bb_re, bb_im = pl.pallas_call(
        _ssmprep_kernel, out_shape=[out] * 4, name="ssmprep",
    )(rep(a_re), rep(a_im), rep(log_dt.reshape(g, 1)), rows(b_re), rows(b_im))
    return ab_re[::c], ab_im[::c], bb_re.reshape(g, c, p), bb_im.reshape(g, c, p)


def _block_diag(w, gpb):
    g, r, k = w.shape
    w = w.reshape(g // gpb, gpb, r, k)
    eye = jnp.eye(gpb, dtype=w.dtype)
    return jnp.einsum("bgrk,gh->bgrhk", w, eye).reshape(g // gpb, gpb * r, gpb * k)


def _ssm_kernel(u_ref, h0r_ref, h0i_ref, bmat_ref, ar_ref, ai_ref, cmat_ref, d_ref,
                y_ref, xr_ref, xi_ref, bu_ref, sr_ref, si_ref, *, tt, nb, ns, precise):
    t = pl.program_id(1)

    @pl.when(t == 0)
    def _():
        sr_ref[...] = h0r_ref[...]
        si_ref[...] = h0i_ref[...]

    u = u_ref[...].reshape(tt * nb, u_ref.shape[-1])
    if precise:
        bu_ref[...] = _dot3(u, bmat_ref[0])
    else:
        bu_ref[...] = _dot(u.astype(BF16), bmat_ref[0].astype(BF16))
    ar = jnp.broadcast_to(ar_ref[0], (nb, ns))
    ai = jnp.broadcast_to(ai_ref[0], (nb, ns))

    def step(k, carry):
        xr, xi = carry
        rows = pl.ds(pl.multiple_of(k * nb, nb), nb)
        nr = ar * xr - ai * xi + bu_ref[rows, :ns]
        ni = ar * xi + ai * xr + bu_ref[rows, ns:]
        bu_ref[rows, :ns] = nr
        bu_ref[rows, ns:] = ni
        return nr, ni

    xr, xi = lax.fori_loop(0, tt, step, (sr_ref[...], si_ref[...]))
    sr_ref[...] = xr
    si_ref[...] = xi
    if precise:
        y = _dot3(bu_ref[...], cmat_ref[0])
    else:
        y = _dot(bu_ref[...].astype(BF16), cmat_ref[0].astype(BF16))
    y = y + d_ref[...] * u
    y_ref[...] = y.reshape(y_ref.shape).astype(y_ref.dtype)

    @pl.when(t == pl.num_programs(1) - 1)
    def _():
        xr_ref[...] = xr
        xi_ref[...] = xi


def _ssm_scan(u, h0_re, h0_im, bmat, ab_re, ab_im, cmat, d_skip, *, tt, precise):
    s, nb, d = u.shape
    nblk, ch, ns2 = bmat.shape
    ns = ns2 // 2
    assert s % tt == 0 and d == nblk * ch
    grid = (nblk, s // tt)
    state_spec = pl.BlockSpec((nb, ns), lambda g, t: (0, g))
    return pl.pallas_call(
        functools.partial(_ssm_kernel, tt=tt, nb=nb, ns=ns, precise=precise),
        grid=grid,
        in_specs=[pl.BlockSpec((tt, nb, ch), lambda g, t: (t, 0, g)),
                  state_spec, state_spec,
                  pl.BlockSpec((1, ch, ns2), lambda g, t: (g, 0, 0)),
                  pl.BlockSpec((1, 1, ns), lambda g, t: (g, 0, 0)),
                  pl.BlockSpec((1, 1, ns), lambda g, t: (g, 0, 0)),
                  pl.BlockSpec((1, ns2, ch), lambda g, t: (g, 0, 0)),
                  pl.BlockSpec((1, ch), lambda g, t: (0, g))],
        out_specs=[pl.BlockSpec((tt, nb, ch), lambda g, t: (t, 0, g)), state_spec, state_spec],
        out_shape=[jax.ShapeDtypeStruct((s, nb, d), F32),
                   jax.ShapeDtypeStruct(h0_re.shape, F32),
                   jax.ShapeDtypeStruct(h0_re.shape, F32)],
        scratch_shapes=[pltpu.VMEM((tt * nb, ns2), F32),
                        pltpu.VMEM((nb, ns), F32), pltpu.VMEM((nb, ns), F32)],
        compiler_params=_params("parallel", "arbitrary"),
        name="ssm",
    )(u, h0_re, h0_im, bmat, ab_re, ab_im, cmat, d_skip)


def _post_kernel(x_ref, gate_ref, y_ref, o_ref, gatt_ref, gssm_ref, matt_ref, mssm_ref,
                 wglu_ref, bglu_ref, watt_ref, wssm_ref, wout_ref, lng_ref, lnb_ref, out_ref, *, alpha):
    g = jax.nn.gelu(y_ref[0].astype(F32))
    glu = g * jax.nn.sigmoid(_dot(g.astype(BF16), wglu_ref[...]) + bglu_ref[...])
    ys = glu * jax.nn.silu(gssm_ref[0].astype(F32))
    y_ssm = _dot(ys.astype(BF16), wssm_ref[...])
    ya = o_ref[0].astype(F32) * jax.nn.silu(gatt_ref[0].astype(F32))
    y_att = _dot(ya.astype(BF16), watt_ref[...])
    merged = (jax.nn.sigmoid(matt_ref[0].astype(F32)) * y_att
              + jax.nn.sigmoid(mssm_ref[0].astype(F32)) * y_ssm)
    r = alpha * x_ref[0] + gate_ref[0] * _dot(merged.astype(BF16), wout_ref[...])
    mu = jnp.mean(r, axis=-1, keepdims=True)
    cen = r - mu
    var = jnp.mean(cen * cen, axis=-1, keepdims=True)
    out_ref[0] = cen * lax.rsqrt(var + LN_EPS) * lng_ref[...] + lnb_ref[...]


def _post(x, gate, y, o_att, z, cols, w_glu, b_glu, w_att_out, w_ssm_out, w_out, ln_g, ln_b, *, tm, alpha):
    b, s, d = x.shape
    per_row = gate.shape[1] != 1
    row = lambda col: pl.BlockSpec((1, tm, d), lambda bi, m, col=col: (bi, m, col))
    gate_spec = row(0) if per_row else pl.BlockSpec((1, 1, d), lambda bi, m: (bi, 0, 0))
    mat = pl.BlockSpec((d, d), lambda bi, m: (0, 0))
    vec = pl.BlockSpec((1, d), lambda bi, m: (0, 0))
    return pl.pallas_call(
        functools.partial(_post_kernel, alpha=alpha),
        grid=(b, s // tm),
        in_specs=[row(0), gate_spec, row(0), row(0)] + [row(c) for c in cols]
                 + [mat, vec, mat, mat, mat, vec, vec],
        out_specs=row(0),
        out_shape=jax.ShapeDtypeStruct((b, s, d), F32),
        compiler_params=_params("parallel", "parallel"),
        name="post",
    )(x, gate, y, o_att, z, z, z, z,
      w_glu.astype(BF16), b_glu.reshape(1, d), w_att_out.astype(BF16), w_ssm_out.astype(BF16),
      w_out.astype(BF16), ln_g.reshape(1, d), ln_b.reshape(1, d))


def _pick(n, target):
    t = min(n, target)
    while n % t:
        t //= 2
    return t


def kernel(x_prompt, x_sample, c_prompt, c_sample, cache_k, cache_v, state_ssm_re, state_ssm_im, page_table, w_cond, b_cond, w_in, sb_bias, ssm_a_re, ssm_a_im, ssm_log_dt, ssm_b_re, ssm_b_im, ssm_c_re, ssm_c_im, ssm_d, w_glu, b_glu, w_att_out, w_ssm_out, w_out, ln_g, ln_b):
    depth = w_in.shape[0]
    assert depth == 1, "single-layer trunk"
    bsz, seq, d = x_prompt.shape
    dbsz, dseq, _ = x_sample.shape
    assert dseq == 1 and w_in.shape[2] == 8 * d
    n_heads = sb_bias.shape[-1]
    hd = cache_k.shape[-1]
    n_groups, n_state = ssm_a_re.shape[1:]
    alpha = (2.0 * depth) ** 0.25
    l = 0

    mod = _modulation(jnp.concatenate([c_prompt, c_sample], axis=0), w_cond[l], b_cond[l])
    shift, scale, gate = mod[:, :d], mod[:, d:2 * d], mod[:, 2 * d:]
    row3 = lambda a, lo, hi, shape: a[lo:hi].reshape(shape)
    p_shape, s_shape = (bsz, 1, d), (1, dbsz, d)

    ab_re, ab_im, bb_re, bb_im = _ssm_discretize(ssm_a_re[l], ssm_a_im[l], ssm_log_dt[l],
                                                 ssm_b_re[l], ssm_b_im[l])
    gpb = GROUPS_PER_BLOCK
    nblk = n_groups // gpb
    bmat = jnp.concatenate([_block_diag(bb_re, gpb), _block_diag(bb_im, gpb)], axis=-1)
    cmat = jnp.concatenate([_block_diag(ssm_c_re[l].transpose(0, 2, 1), gpb),
                            _block_diag(-ssm_c_im[l].transpose(0, 2, 1), gpb)], axis=1)
    a_blk = lambda a: a.reshape(nblk, 1, gpb * n_state)
    d_skip = ssm_d[l].reshape(1, d)

    wg = [w_in[l][:, i * d:(i + 1) * d] for i in range(8)]

    w_p = jnp.concatenate([wg[1].T, wg[2].T, wg[0]] + wg[3:], axis=1).astype(BF16)
    kvt_p, z_p = _inproj(x_prompt, row3(scale, 0, bsz, p_shape), row3(shift, 0, bsz, p_shape), w_p,
                         nt=2, tm=_pick(seq, 512), precise=False, z_dtype=BF16)
    o_p = _prompt_attention(z_p, kvt_p, sb_bias[l], d=d, hd=hd, tq=_pick(seq, 256), nh=4)
    u_p = z_p[:, :, 2 * d:3 * d].astype(F32).transpose(1, 0, 2)
    zeros = jnp.zeros((bsz, n_groups * n_state), F32)
    y_p, sr_p, si_p = _ssm_scan(u_p, zeros, zeros, bmat, a_blk(ab_re), a_blk(ab_im), cmat, d_skip,
                                tt=_pick(seq, 256), precise=False)
    out_p = _post(x_prompt, row3(gate, 0, bsz, p_shape), y_p.transpose(1, 0, 2), o_p, z_p, (1, 3, 4, 5),
                  w_glu[l], b_glu[l], w_att_out[l], w_ssm_out[l], w_out[l], ln_g[l], ln_b[l],
                  tm=_pick(seq, 512), alpha=alpha)

    x_s = x_sample.reshape(1, dbsz, d)
    w_s = jnp.concatenate([wg[1].T, wg[2].T, wg[0].T] + wg[3:], axis=1)
    kvqt_s, z_s = _inproj(x_s, row3(scale, bsz, bsz + dbsz, s_shape), row3(shift, bsz, bsz + dbsz, s_shape),
                          w_s, nt=3, tm=dbsz, precise=True, z_dtype=F32)
    pool = lambda c: c[l].transpose(0, 2, 3, 1)
    o_s = _decode_attention(kvqt_s[2, 0], pool(cache_k), pool(cache_v), page_table, sb_bias[l],
                            npg=_pick(page_table.shape[1], 8))
    y_s, sr_s, si_s = _ssm_scan(z_s[:, :, d:2 * d], state_ssm_re[l].reshape(dbsz, -1),
                                state_ssm_im[l].reshape(dbsz, -1), bmat, a_blk(ab_re), a_blk(ab_im),
                                cmat, d_skip, tt=1, precise=True)
    out_s = _post(x_s, row3(gate, bsz, bsz + dbsz, s_shape), y_s, o_s.reshape(1, dbsz, d), z_s, (0, 2, 3, 4),
                  w_glu[l], b_glu[l], w_att_out[l], w_ssm_out[l], w_out[l], ln_g[l], ln_b[l],
                  tm=dbsz, alpha=alpha)

    heads_t = lambda t, b, s: t.reshape(1, b, n_heads, hd, s).transpose(0, 1, 4, 2, 3)
    state = lambda t, b: t.reshape(1, b, n_groups, n_state)
    return (out_p, out_s.reshape(dbsz, 1, d),
            heads_t(kvt_p[0], bsz, seq), heads_t(kvt_p[1], bsz, seq), state(sr_p, bsz), state(si_p, bsz),
            heads_t(kvqt_s[0], 1, dbsz).reshape(1, dbsz, 1, n_heads, hd),
            heads_t(kvqt_s[1], 1, dbsz).reshape(1, dbsz, 1, n_heads, hd),
            state(sr_s, dbsz), state(si_s, dbsz))
```

```python
import functools
import math

import jax
import jax.numpy as jnp
from jax import lax
from jax.experimental import pallas as pl
from jax.experimental.pallas import tpu as pltpu

F32 = jnp.float32
BF16 = jnp.bfloat16
LN_EPS = 1e-5
LOG2E = math.log2(math.e)
LANES = 128
GROUPS_PER_BLOCK = 8
VMEM_LIMIT = 56 * 1024 * 1024
NT_DIMS = (((1,), (1,)), ((), ()))


def _dot(a, b, dims=None):
    if dims is None:
        return jnp.dot(a, b, preferred_element_type=F32)
    return lax.dot_general(a, b, dims, preferred_element_type=F32)


def _split(a):
    hi = a.astype(BF16)
    lo = (a - hi.astype(F32)).astype(BF16)
    return hi, lo


def _dot3(a, b, dims=None):
    ah, al = _split(a)
    bh, bl = _split(b)
    return _dot(ah, bh, dims) + (_dot(ah, bl, dims) + _dot(al, bh, dims))


def _dot_hilo(a, b_bf16):
    hi, lo = _split(a)
    return _dot(hi, b_bf16) + _dot(lo, b_bf16)


def _softplus2(y):
    return jnp.maximum(y, 0.0) + jnp.log2(1.0 + jnp.exp2(-jnp.abs(y)))


def _params(*sem):
    return pltpu.CompilerParams(dimension_semantics=sem, vmem_limit_bytes=VMEM_LIMIT)


def _mod_kernel(c_ref, w_ref, b_ref, o_ref):
    o_ref[...] = _dot3(c_ref[...], w_ref[...]) + b_ref[...]


def _modulation(c, w_cond, b_cond):
    rows, d = c.shape
    n = w_cond.shape[1]
    tn = 768 if n % 768 == 0 else n
    return pl.pallas_call(
        _mod_kernel,
        grid=(n // tn,),
        in_specs=[pl.BlockSpec((rows, d), lambda j: (0, 0)),
                  pl.BlockSpec((d, tn), lambda j: (0, j)),
                  pl.BlockSpec((1, tn), lambda j: (0, j))],
        out_specs=pl.BlockSpec((rows, tn), lambda j: (0, j)),
        out_shape=jax.ShapeDtypeStruct((rows, n), F32),
        compiler_params=_params("arbitrary"),
        name="mod",
    )(c, w_cond, b_cond.reshape(1, n))


def _inproj_prompt_kernel(x_ref, scale_ref, shift_ref, w_ref, wt_ref, kt_ref, vt_ref, z_ref, *, d, cols):
    h = (x_ref[0] * (1.0 + scale_ref[0]) + shift_ref[0]).astype(BF16)
    kt_ref[0] = _dot(wt_ref[0], h, NT_DIMS)
    vt_ref[0] = _dot(wt_ref[1], h, NT_DIMS)
    for gi, c in enumerate(cols):
        z_ref[0, :, gi * d:(gi + 1) * d] = _dot(h, w_ref[:, c * d:(c + 1) * d]).astype(z_ref.dtype)


def _inproj_prompt(x, scale, shift, w, wt, *, tm, cols):
    b, s, d = x.shape
    assert s % tm == 0
    resident = lambda shape: pl.BlockSpec(shape, lambda bi, m: (0,) * len(shape),
                                          pipeline_mode=pl.Buffered(1))
    mod_spec = pl.BlockSpec((1, 1, d), lambda bi, m: (bi, 0, 0))
    t_spec = pl.BlockSpec((1, d, tm), lambda bi, m: (bi, 0, m))
    return pl.pallas_call(
        functools.partial(_inproj_prompt_kernel, d=d, cols=cols),
        grid=(b, s // tm),
        in_specs=[pl.BlockSpec((1, tm, d), lambda bi, m: (bi, m, 0)), mod_spec, mod_spec,
                  resident(w.shape), resident(wt.shape)],
        out_specs=[t_spec, t_spec, pl.BlockSpec((1, tm, len(cols) * d), lambda bi, m: (bi, m, 0))],
        out_shape=[jax.ShapeDtypeStruct((b, d, s), F32), jax.ShapeDtypeStruct((b, d, s), F32),
                   jax.ShapeDtypeStruct((b, s, len(cols) * d), BF16)],
        compiler_params=_params("parallel", "parallel"),
        name="inproj_prompt",
    )(x, scale, shift, w, wt)


def _inproj_sample_kernel(x_ref, scale_ref, shift_ref, w_ref, wt_ref, t_ref, z_ref, h_ref, *, nt):
    j = pl.program_id(0)

    @pl.when(j == 0)
    def _():
        h_ref[...] = x_ref[...] * (1.0 + scale_ref[...]) + shift_ref[...]

    @pl.when(j < nt)
    def _():
        t_ref[0] = _dot3(wt_ref[0], h_ref[...], NT_DIMS)

    @pl.when(j >= nt)
    def _():
        z_ref[...] = _dot3(h_ref[...], w_ref[...])


def _inproj_sample(x, scale, shift, w, wt):
    r, d = x.shape
    ng, nt = w.shape[1] // d, wt.shape[0]
    rows = pl.BlockSpec((r, d), lambda j: (0, 0))
    return pl.pallas_call(
        functools.partial(_inproj_sample_kernel, nt=nt),
        grid=(ng,),
        in_specs=[rows, rows, rows,
                  pl.BlockSpec((d, d), lambda j: (0, jnp.maximum(j, nt))),
                  pl.BlockSpec((1, d, d), lambda j: (jnp.minimum(j, nt - 1), 0, 0))],
        out_specs=[pl.BlockSpec((1, d, r), lambda j: (jnp.minimum(j, nt - 1), 0, 0)),
                   pl.BlockSpec((r, d), lambda j: (0, jnp.maximum(j - nt, 0)))],
        out_shape=[jax.ShapeDtypeStruct((nt, d, r), F32),
                   jax.ShapeDtypeStruct((r, (ng - nt) * d), F32)],
        scratch_shapes=[pltpu.VMEM((r, d), F32)],
        compiler_params=_params("arbitrary"),
        name="inproj_sample",
    )(x, scale, shift, w, wt)


def _attn_kernel(bias_ref, q_ref, kt_ref, vt_ref, tri_ref, o_ref, y_ref, incl_ref, acc_ref, run_ref,
                 *, tq, hd, nh):
    g = pl.program_id(1)
    i = pl.program_id(2)
    q = q_ref[0]
    tri = tri_ref[...]
    row = lax.broadcasted_iota(jnp.int32, (tq, tq), 0)
    col = lax.broadcasted_iota(jnp.int32, (tq, tq), 1)
    below_diag = col < row
    logit_scale = hd ** -0.5 * LOG2E

    def block_start(jj):
        return pl.multiple_of((i - jj) * tq, tq)

    def logits(jj, slot, diagonal=False):
        for hh in range(nh):
            rows = slice(hh * hd, (hh + 1) * hd)
            kt = kt_ref[0, rows, pl.ds(block_start(jj), tq)].astype(BF16)
            y = _dot(q[:, rows], kt) * logit_scale + bias_ref[g * nh + hh] * LOG2E
            if diagonal:
                y = jnp.where(below_diag, y, -1e30)
            hi, lo = _split(_softplus2(y))
            y_ref[slot, hh] = y
            incl_ref[slot, hh] = _dot(jnp.concatenate([hi, lo], axis=1), tri)

    def weights(jj, slot):
        for hh in range(nh):
            rows = slice(hh * hd, (hh + 1) * hd)
            vt = vt_ref[0, rows, pl.ds(block_start(jj), tq)].astype(BF16)
            incl = incl_ref[slot, hh]
            run = run_ref[hh]
            w = jnp.exp2(y_ref[slot, hh] - incl - run)
            acc_ref[hh] += _dot(w.astype(BF16), vt, NT_DIMS)
            run_ref[hh] = run + incl[:, :1]

    acc_ref[...] = jnp.zeros_like(acc_ref)
    run_ref[...] = jnp.zeros_like(run_ref)
    logits(0, 0, diagonal=True)

    @pl.loop(0, i // 2)
    def _(m):
        logits(2 * m + 1, 1)
        weights(2 * m, 0)
        logits(2 * m + 2, 0)
        weights(2 * m + 1, 1)

    @pl.when(i % 2 == 1)
    def _():
        logits(i, 1)
        weights(i - 1, 0)
        weights(i, 1)

    @pl.when(i % 2 == 0)
    def _():
        weights(i, 0)

    o_ref[0] = jnp.concatenate([acc_ref[hh] for hh in range(nh)], axis=-1).astype(o_ref.dtype)


def _prompt_attention(z, kt, vt, sb_bias, *, d, hd, tq, nh):
    b, s, _ = z.shape
    width = nh * hd
    r = lax.broadcasted_iota(jnp.int32, (2 * tq, tq), 0) % tq
    c = lax.broadcasted_iota(jnp.int32, (2 * tq, tq), 1)
    tri = (r >= c).astype(BF16)
    return pl.pallas_call(
        functools.partial(_attn_kernel, tq=tq, hd=hd, nh=nh),
        grid=(b, d // width, s // tq),
        in_specs=[pl.BlockSpec(memory_space=pltpu.SMEM),
                  pl.BlockSpec((1, tq, width), lambda bi, g, i: (bi, i, g)),
                  pl.BlockSpec((1, width, s), lambda bi, g, i: (bi, g, 0)),
                  pl.BlockSpec((1, width, s), lambda bi, g, i: (bi, g, 0)),
                  pl.BlockSpec((2 * tq, tq), lambda bi, g, i: (0, 0))],
        out_specs=pl.BlockSpec((1, tq, width), lambda bi, g, i: (bi, i, g)),
        out_shape=jax.ShapeDtypeStruct((b, s, d), BF16),
        scratch_shapes=[pltpu.VMEM((2, nh, tq, tq), F32), pltpu.VMEM((2, nh, tq, tq), F32),
                        pltpu.VMEM((nh, tq, hd), F32), pltpu.VMEM((nh, tq, 1), F32)],
        compiler_params=_params("parallel", "parallel", "arbitrary"),
        name="attn",
    )(sb_bias.astype(F32), z, kt, vt, tri)


def _decode_kernel(pt_ref, qt_ref, bias_ref, tri_ref, *refs, hd, npg):
    k_refs, v_refs = refs[:npg], refs[npg:2 * npg]
    o_ref, qb_ref, acc_ref, run_ref = refs[2 * npg:]
    bi = pl.program_id(0)
    p = pl.program_id(1)
    nh, _, page = acc_ref.shape

    @pl.when(p == 0)
    def _():
        nb = qt_ref.shape[1]
        onehot = (lax.broadcasted_iota(jnp.int32, (nb, page), 0) == bi).astype(BF16)
        qb = _dot_hilo(qt_ref[...], onehot) * (hd ** -0.5 * LOG2E)
        qb_ref[...] = qb.reshape(qb_ref.shape)
        acc_ref[...] = jnp.zeros_like(acc_ref)
        run_ref[...] = jnp.zeros_like(run_ref)

    qb = qb_ref[...]
    bias2 = bias_ref[...] * LOG2E
    tri = tri_ref[...]
    run = run_ref[...]
    weights = []
    for kr in k_refs:
        y = jnp.sum(kr[0] * qb, axis=1) + bias2
        sp = _softplus2(y)
        ext = _dot_hilo(sp, tri)
        weights.append(jnp.exp2(y - sp - ext[:, :page] - run))
        run = run + ext[:, page:]
    run_ref[...] = run
    for h in range(nh):
        part = v_refs[0][0, h] * weights[0][h:h + 1, :]
        for i in range(1, npg):
            part = part + v_refs[i][0, h] * weights[i][h:h + 1, :]
        acc_ref[h] += part

    @pl.when(p == pl.num_programs(1) - 1)
    def _():
        o_ref[0] = jnp.sum(acc_ref[...], axis=-1)


def _decode_attention(qt, cache_kt, cache_vt, page_table, sb_bias, *, npg):
    d, b = qt.shape
    _, nh, hd, page = cache_kt.shape
    n_pages = page_table.shape[1]
    assert n_pages % npg == 0 and page == LANES
    r = lax.broadcasted_iota(jnp.int32, (page, 2 * page), 0)
    c = lax.broadcasted_iota(jnp.int32, (page, 2 * page), 1)
    tri = jnp.logical_or(r > c, c >= page).astype(BF16)
    bias = jnp.broadcast_to(sb_bias.astype(F32)[:, None], (nh, page))

    def page_spec(i):
        return pl.BlockSpec((1, nh, hd, page),
                            lambda bi, p, pt: (pt[bi, n_pages - 1 - (p * npg + i)], 0, 0, 0))

    const = lambda bi, p, pt: (0, 0)
    grid_spec = pltpu.PrefetchScalarGridSpec(
        num_scalar_prefetch=1,
        grid=(b, n_pages // npg),
        in_specs=[pl.BlockSpec((d, b), const), pl.BlockSpec((nh, page), const),
                  pl.BlockSpec((page, 2 * page), const)]
                 + [page_spec(i) for i in range(npg)] * 2,
        out_specs=pl.BlockSpec((1, nh, hd), lambda bi, p, pt: (bi, 0, 0)),
        scratch_shapes=[pltpu.VMEM((nh, hd, page), F32), pltpu.VMEM((nh, hd, page), F32),
                        pltpu.VMEM((nh, page), F32)],
    )
    return pl.pallas_call(
        functools.partial(_decode_kernel, hd=hd, npg=npg),
        grid_spec=grid_spec,
        out_shape=jax.ShapeDtypeStruct((b, nh, hd), F32),
        compiler_params=_params("parallel", "arbitrary"),
        name="decode",
    )(page_table, qt, bias, tri, *([cache_kt] * npg), *([cache_vt] * npg))


def _ssmprep_kernel(ar_ref, ai_ref, ldt_ref, br_ref, bi_ref, abr_ref, abi_ref, bbr_ref, bbi_ref):
    dt = jnp.exp(ldt_ref[...])
    lr, li = ar_ref[...], ai_ref[...]
    mag = jnp.exp(lr * dt)
    ab_re, ab_im = mag * jnp.cos(li * dt), mag * jnp.sin(li * dt)
    den = lr * lr + li * li
    nr = ab_re - 1.0
    co_re = (nr * lr + ab_im * li) / den
    co_im = (ab_im * lr - nr * li) / den
    abr_ref[...] = ab_re
    abi_ref[...] = ab_im
    br, bi = br_ref[...], bi_ref[...]
    bbr_ref[...] = co_re * br - co_im * bi
    bbi_ref[...] = co_re * bi + co_im * br


def _ssm_discretize(a_re, a_im, log_dt, b_re, b_im):
    g, p = a_re.shape
    c = b_re.shape[-1]
    rep = lambda a: jnp.repeat(a, c, axis=0)
    rows = lambda b: b.transpose(0, 2, 1).reshape(g * c, p)
    out = jax.ShapeDtypeStruct((g * c, p), F32)
    ab_re, ab_im, bb_re, bb_im = pl.pallas_call(
        _ssmprep_kernel, out_shape=[out] * 4, name="ssmprep",
    )(rep(a_re), rep(a_im), rep(log_dt.reshape(g, 1)), rows(b_re), rows(b_im))
    return ab_re[::c], ab_im[::c], bb_re.reshape(g, c, p), bb_im.reshape(g, c, p)


def _block_diag(w, gpb):
    g, r, k = w.shape
    w = w.reshape(g // gpb, gpb, r, k)
    eye = jnp.eye(gpb, dtype=w.dtype)
    return jnp.einsum("bgrk,gh->bgrhk", w, eye).reshape(g // gpb, gpb * r, gpb * k)


def _ssm_kernel(u_ref, h0r_ref, h0i_ref, bmat_ref, ar_ref, ai_ref, cmat_ref, d_ref,
                y_ref, xr_ref, xi_ref, bu_ref, sr_ref, si_ref, *, tt, nb, ns, precise):
    t = pl.program_id(1)

    @pl.when(t == 0)
    def _():
        sr_ref[...] = h0r_ref[...]
        si_ref[...] = h0i_ref[...]

    u = u_ref[...].reshape(tt * nb, u_ref.shape[-1])
    if precise:
        bu_ref[...] = _dot3(u, bmat_ref[0])
    else:
        bu_ref[...] = _dot(u.astype(BF16), bmat_ref[0].astype(BF16))
    ar = jnp.broadcast_to(ar_ref[0], (nb, ns))
    ai = jnp.broadcast_to(ai_ref[0], (nb, ns))

    def step(k, carry):
        xr, xi = carry
        rows = pl.ds(pl.multiple_of(k * nb, nb), nb)
        nr = ar * xr - ai * xi + bu_ref[rows, :ns]
        ni = ar * xi + ai * xr + bu_ref[rows, ns:]
        bu_ref[rows, :ns] = nr
        bu_ref[rows, ns:] = ni
        return nr, ni

    xr, xi = lax.fori_loop(0, tt, step, (sr_ref[...], si_ref[...]))
    sr_ref[...] = xr
    si_ref[...] = xi
    if precise:
        y = _dot3(bu_ref[...], cmat_ref[0])
    else:
        y = _dot(bu_ref[...].astype(BF16), cmat_ref[0].astype(BF16))
    y = y + d_ref[...] * u
    y_ref[...] = y.reshape(y_ref.shape).astype(y_ref.dtype)

    @pl.when(t == pl.num_programs(1) - 1)
    def _():
        xr_ref[...] = xr
        xi_ref[...] = xi


def _ssm_scan(u, h0_re, h0_im, bmat, ab_re, ab_im, cmat, d_skip, *, tt, precise):
    s, nb, d = u.shape
    nblk, ch, ns2 = bmat.shape
    ns = ns2 // 2
    assert s % tt == 0 and d == nblk * ch
    grid = (nblk, s // tt)
    state_spec = pl.BlockSpec((nb, ns), lambda g, t: (0, g))
    return pl.pallas_call(
        functools.partial(_ssm_kernel, tt=tt, nb=nb, ns=ns, precise=precise),
        grid=grid,
        in_specs=[pl.BlockSpec((tt, nb, ch), lambda g, t: (t, 0, g)),
                  state_spec, state_spec,
                  pl.BlockSpec((1, ch, ns2), lambda g, t: (g, 0, 0)),
                  pl.BlockSpec((1, 1, ns), lambda g, t: (g, 0, 0)),
                  pl.BlockSpec((1, 1, ns), lambda g, t: (g, 0, 0)),
                  pl.BlockSpec((1, ns2, ch), lambda g, t: (g, 0, 0)),
                  pl.BlockSpec((1, ch), lambda g, t: (0, g))],
        out_specs=[pl.BlockSpec((tt, nb, ch), lambda g, t: (t, 0, g)), state_spec, state_spec],
        out_shape=[jax.ShapeDtypeStruct((s, nb, d), F32),
                   jax.ShapeDtypeStruct(h0_re.shape, F32),
                   jax.ShapeDtypeStruct(h0_re.shape, F32)],
        scratch_shapes=[pltpu.VMEM((tt * nb, ns2), F32),
                        pltpu.VMEM((nb, ns), F32), pltpu.VMEM((nb, ns), F32)],
        compiler_params=_params("parallel", "arbitrary"),
        name="ssm",
    )(u, h0_re, h0_im, bmat, ab_re, ab_im, cmat, d_skip)


def _ssm_prompt_kernel(u_ref, bmat_ref, ar_ref, ai_ref, cmat_ref, d_ref, y_ref, xr_ref, xi_ref,
                       bu_ref, sr_ref, si_ref, *, tt, nb, ns, pitch):
    t = pl.program_id(1)

    @pl.when(t == 0)
    def _():
        sr_ref[...] = jnp.zeros_like(sr_ref)
        si_ref[...] = jnp.zeros_like(si_ref)

    nct = bu_ref.shape[0]
    half = nct // 2
    bmat = bmat_ref[0].astype(BF16)
    for b in range(nb):
        bu = _dot(u_ref[b], bmat)
        for c in range(nct):
            bu_ref[c, pl.ds(b * pitch, tt), :] = bu[:, c * LANES:(c + 1) * LANES]
    ar = jnp.broadcast_to(ar_ref[0], (nb, ns))
    ai = jnp.broadcast_to(ai_ref[0], (nb, ns))
    lanes = lambda a, c: a[:, c * LANES:(c + 1) * LANES]

    def step(k, carry):
        xr, xi = carry
        rows = pl.ds(k, nb, stride=pitch)
        nr, ni = [], []
        for c in range(half):
            r = lanes(ar, c) * lanes(xr, c) - lanes(ai, c) * lanes(xi, c) + bu_ref[c, rows, :]
            i = lanes(ar, c) * lanes(xi, c) + lanes(ai, c) * lanes(xr, c) + bu_ref[half + c, rows, :]
            bu_ref[c, rows, :] = r
            bu_ref[half + c, rows, :] = i
            nr.append(r)
            ni.append(i)
        return jnp.concatenate(nr, axis=1), jnp.concatenate(ni, axis=1)

    xr, xi = lax.fori_loop(0, tt, step, (sr_ref[...], si_ref[...]))
    sr_ref[...] = xr
    si_ref[...] = xi
    cmat = cmat_ref[0].astype(BF16)
    for b in range(nb):
        xb = jnp.concatenate([bu_ref[c, pl.ds(b * pitch, tt), :] for c in range(nct)], axis=1)
        y = _dot(xb.astype(BF16), cmat) + d_ref[...] * u_ref[b].astype(F32)
        y_ref[b] = y.astype(y_ref.dtype)

    @pl.when(t == pl.num_programs(1) - 1)
    def _():
        xr_ref[...] = xr
        xi_ref[...] = xi


def _ssm_prompt(z, ucol, bmat, ab_re, ab_im, cmat, d_skip, *, tt):
    nb, s, _ = z.shape
    nblk, ch, ns2 = bmat.shape
    ns = ns2 // 2
    d = nblk * ch
    assert s % tt == 0 and nb % 8 == 0
    pitch = tt + 8
    state_spec = pl.BlockSpec((nb, ns), lambda g, t: (0, g))
    state_shape = jax.ShapeDtypeStruct((nb, nblk * ns), F32)
    return pl.pallas_call(
        functools.partial(_ssm_prompt_kernel, tt=tt, nb=nb, ns=ns, pitch=pitch),
        grid=(nblk, s // tt),
        in_specs=[pl.BlockSpec((nb, tt, ch), lambda g, t: (0, t, ucol * nblk + g)),
                  pl.BlockSpec((1, ch, ns2), lambda g, t: (g, 0, 0)),
                  pl.BlockSpec((1, 1, ns), lambda g, t: (g, 0, 0)),
                  pl.BlockSpec((1, 1, ns), lambda g, t: (g, 0, 0)),
                  pl.BlockSpec((1, ns2, ch), lambda g, t: (g, 0, 0)),
                  pl.BlockSpec((1, ch), lambda g, t: (0, g))],
        out_specs=[pl.BlockSpec((nb, tt, ch), lambda g, t: (0, t, g)), state_spec, state_spec],
        out_shape=[jax.ShapeDtypeStruct((nb, s, d), BF16), state_shape, state_shape],
        scratch_shapes=[pltpu.VMEM((ns2 // LANES, nb * pitch, LANES), F32),
                        pltpu.VMEM((nb, ns), F32), pltpu.VMEM((nb, ns), F32)],
        compiler_params=_params("parallel", "arbitrary"),
        name="ssm_prompt",
    )(z, bmat, ab_re, ab_im, cmat, d_skip)


def _post_kernel(x_ref, gate_ref, y_ref, o_ref, gatt_ref, gssm_ref, matt_ref, mssm_ref,
                 wglu_ref, bglu_ref, watt_ref, wssm_ref, wout_ref, lng_ref, lnb_ref, out_ref, *, alpha):
    g = jax.nn.gelu(y_ref[0].astype(F32))
    glu = g * jax.nn.sigmoid(_dot(g.astype(BF16), wglu_ref[...]) + bglu_ref[...])
    ys = glu * jax.nn.silu(gssm_ref[0].astype(F32))
    y_ssm = _dot(ys.astype(BF16), wssm_ref[...])
    ya = o_ref[0].astype(F32) * jax.nn.silu(gatt_ref[0].astype(F32))
    y_att = _dot(ya.astype(BF16), watt_ref[...])
    merged = (jax.nn.sigmoid(matt_ref[0].astype(F32)) * y_att
              + jax.nn.sigmoid(mssm_ref[0].astype(F32)) * y_ssm)
    r = alpha * x_ref[0] + gate_ref[0] * _dot(merged.astype(BF16), wout_ref[...])
    mu = jnp.mean(r, axis=-1, keepdims=True)
    cen = r - mu
    var = jnp.mean(cen * cen, axis=-1, keepdims=True)
    out_ref[0] = cen * lax.rsqrt(var + LN_EPS) * lng_ref[...] + lnb_ref[...]


def _post(x, gate, y, o_att, z, cols, w_glu, b_glu, w_att_out, w_ssm_out, w_out, ln_g, ln_b, *, tm, alpha):
    b, s, d = x.shape
    per_row = gate.shape[1] != 1
    row = lambda col: pl.BlockSpec((1, tm, d), lambda bi, m, col=col: (bi, m, col))
    gate_spec = row(0) if per_row else pl.BlockSpec((1, 1, d), lambda bi, m: (bi, 0, 0))
    mat = pl.BlockSpec((d, d), lambda bi, m: (0, 0))
    vec = pl.BlockSpec((1, d), lambda bi, m: (0, 0))
    return pl.pallas_call(
        functools.partial(_post_kernel, alpha=alpha),
        grid=(b, s // tm),
        in_specs=[row(0), gate_spec, row(0), row(0)] + [row(c) for c in cols]
                 + [mat, vec, mat, mat, mat, vec, vec],
        out_specs=row(0),
        out_shape=jax.ShapeDtypeStruct((b, s, d), F32),
        compiler_params=_params("parallel", "parallel"),
        name="post",
    )(x, gate, y, o_att, z, z, z, z,
      w_glu.astype(BF16), b_glu.reshape(1, d), w_att_out.astype(BF16), w_ssm_out.astype(BF16),
      w_out.astype(BF16), ln_g.reshape(1, d), ln_b.reshape(1, d))


def _pick(n, target):
    t = min(n, target)
    while n % t:
        t //= 2
    return t


def kernel(x_prompt, x_sample, c_prompt, c_sample, cache_k, cache_v, state_ssm_re, state_ssm_im, page_table, w_cond, b_cond, w_in, sb_bias, ssm_a_re, ssm_a_im, ssm_log_dt, ssm_b_re, ssm_b_im, ssm_c_re, ssm_c_im, ssm_d, w_glu, b_glu, w_att_out, w_ssm_out, w_out, ln_g, ln_b):
    depth = w_in.shape[0]
    assert depth == 1, "single-layer trunk"
    bsz, seq, d = x_prompt.shape
    dbsz, dseq, _ = x_sample.shape
    assert dseq == 1 and w_in.shape[2] == 8 * d
    n_heads = sb_bias.shape[-1]
    hd = cache_k.shape[-1]
    n_groups, n_state = ssm_a_re.shape[1:]
    alpha = (2.0 * depth) ** 0.25
    l = 0

    mod = _modulation(jnp.concatenate([c_prompt, c_sample], axis=0), w_cond[l], b_cond[l])
    shift, scale, gate = mod[:, :d], mod[:, d:2 * d], mod[:, 2 * d:]
    row3 = lambda a, lo, hi, shape: a[lo:hi].reshape(shape)
    p_shape, s_shape = (bsz, 1, d), (1, dbsz, d)

    ab_re, ab_im, bb_re, bb_im = _ssm_discretize(ssm_a_re[l], ssm_a_im[l], ssm_log_dt[l],
                                                 ssm_b_re[l], ssm_b_im[l])
    gpb = GROUPS_PER_BLOCK
    nblk = n_groups // gpb
    bmat = jnp.concatenate([_block_diag(bb_re, gpb), _block_diag(bb_im, gpb)], axis=-1)
    cmat = jnp.concatenate([_block_diag(ssm_c_re[l].transpose(0, 2, 1), gpb),
                            _block_diag(-ssm_c_im[l].transpose(0, 2, 1), gpb)], axis=1)
    a_blk = lambda a: a.reshape(nblk, 1, gpb * n_state)
    d_skip = ssm_d[l].reshape(1, d)

    w = w_in[l]
    wt = w[:, :3 * d].reshape(d, 3, d).transpose(1, 2, 0)
    wt_kvq = jnp.stack([wt[1], wt[2], wt[0]])

    kt_p, vt_p, z_p = _inproj_prompt(x_prompt, row3(scale, 0, bsz, p_shape), row3(shift, 0, bsz, p_shape),
                                     w.astype(BF16), wt_kvq[:2].astype(BF16),
                                     tm=_pick(seq, 512), cols=(0, 3, 4, 5, 6, 7))
    o_p = _prompt_attention(z_p, kt_p, vt_p, sb_bias[l], d=d, hd=hd, tq=_pick(seq, 256), nh=4)
    y_p, sr_p, si_p = _ssm_prompt(z_p, 2, bmat, a_blk(ab_re), a_blk(ab_im), cmat, d_skip, tt=_pick(seq, 256))
    out_p = _post(x_prompt, row3(gate, 0, bsz, p_shape), y_p, o_p, z_p, (1, 3, 4, 5),
                  w_glu[l], b_glu[l], w_att_out[l], w_ssm_out[l], w_out[l], ln_g[l], ln_b[l],
                  tm=_pick(seq, 512), alpha=alpha)

    x_s = x_sample.reshape(dbsz, d)
    kvqt_s, z_s = _inproj_sample(x_s, scale[bsz:], shift[bsz:], w, wt_kvq)
    z_s = z_s.reshape(1, dbsz, 5 * d)
    pool = lambda c: c[l].transpose(0, 2, 3, 1)
    o_s = _decode_attention(kvqt_s[2], pool(cache_k), pool(cache_v), page_table, sb_bias[l],
                            npg=_pick(page_table.shape[1], 8))
    y_s, sr_s, si_s = _ssm_scan(z_s[:, :, d:2 * d], state_ssm_re[l].reshape(dbsz, -1),
                                state_ssm_im[l].reshape(dbsz, -1), bmat, a_blk(ab_re), a_blk(ab_im),
                                cmat, d_skip, tt=1, precise=True)
    out_s = _post(x_s.reshape(1, dbsz, d), row3(gate, bsz, bsz + dbsz, s_shape), y_s, o_s.reshape(1, dbsz, d),
                  z_s, (0, 2, 3, 4),
                  w_glu[l], b_glu[l], w_att_out[l], w_ssm_out[l], w_out[l], ln_g[l], ln_b[l],
                  tm=dbsz, alpha=alpha)

    heads_t = lambda t, b, s: t.reshape(1, b, n_heads, hd, s).transpose(0, 1, 4, 2, 3)
    state = lambda t, b: t.reshape(1, b, n_groups, n_state)
    return (out_p, out_s.reshape(dbsz, 1, d),
            heads_t(kt_p, bsz, seq), heads_t(vt_p, bsz, seq), state(sr_p, bsz), state(si_p, bsz),
            heads_t(kvqt_s[0], 1, dbsz).reshape(1, dbsz, 1, n_heads, hd),
            heads_t(kvqt_s[1], 1, dbsz).reshape(1, dbsz, 1, n_heads, hd),
            state(sr_s, dbsz), state(si_s, dbsz))
```

```python
import functools
import math

import jax
import jax.numpy as jnp
from jax import lax
from jax.experimental import pallas as pl
from jax.experimental.pallas import tpu as pltpu

F32 = jnp.float32
BF16 = jnp.bfloat16
LN_EPS = 1e-5
LOG2E = math.log2(math.e)
LANES = 128
GROUPS_PER_BLOCK = 8
VMEM_LIMIT = 56 * 1024 * 1024
NT_DIMS = (((1,), (1,)), ((), ()))


def _dot(a, b, dims=None):
    if dims is None:
        return jnp.dot(a, b, preferred_element_type=F32)
    return lax.dot_general(a, b, dims, preferred_element_type=F32)


def _split(a):
    hi = a.astype(BF16)
    lo = (a - hi.astype(F32)).astype(BF16)
    return hi, lo


def _dot3(a, b, dims=None):
    ah, al = _split(a)
    bh, bl = _split(b)
    return _dot(ah, bh, dims) + (_dot(ah, bl, dims) + _dot(al, bh, dims))


def _dot_hilo(a, b_bf16):
    hi, lo = _split(a)
    return _dot(hi, b_bf16) + _dot(lo, b_bf16)


def _softplus2(y):
    return jnp.maximum(y, 0.0) + jnp.log2(1.0 + jnp.exp2(-jnp.abs(y)))


def _params(*sem):
    return pltpu.CompilerParams(dimension_semantics=sem, vmem_limit_bytes=VMEM_LIMIT)


def _mod_kernel(c_ref, w_ref, b_ref, o_ref):
    o_ref[...] = _dot3(c_ref[...], w_ref[...]) + b_ref[...]


def _modulation(c, w_cond, b_cond):
    rows, d = c.shape
    n = w_cond.shape[1]
    tn = 768 if n % 768 == 0 else n
    return pl.pallas_call(
        _mod_kernel,
        grid=(n // tn,),
        in_specs=[pl.BlockSpec((rows, d), lambda j: (0, 0)),
                  pl.BlockSpec((d, tn), lambda j: (0, j)),
                  pl.BlockSpec((1, tn), lambda j: (0, j))],
        out_specs=pl.BlockSpec((rows, tn), lambda j: (0, j)),
        out_shape=jax.ShapeDtypeStruct((rows, n), F32),
        compiler_params=_params("arbitrary"),
        name="mod",
    )(c, w_cond, b_cond.reshape(1, n))


def _inproj_prompt_kernel(x_ref, scale_ref, shift_ref, w_ref, wt_ref, kt_ref, vt_ref, z_ref, *, d, cols):
    h = (x_ref[0] * (1.0 + scale_ref[0]) + shift_ref[0]).astype(BF16)
    kt_ref[0] = _dot(wt_ref[0], h, NT_DIMS)
    vt_ref[0] = _dot(wt_ref[1], h, NT_DIMS)
    for gi, c in enumerate(cols):
        z_ref[0, :, gi * d:(gi + 1) * d] = _dot(h, w_ref[:, c * d:(c + 1) * d]).astype(z_ref.dtype)


def _inproj_prompt(x, scale, shift, w, wt, *, tm, cols):
    b, s, d = x.shape
    assert s % tm == 0
    resident = lambda shape: pl.BlockSpec(shape, lambda bi, m: (0,) * len(shape),
                                          pipeline_mode=pl.Buffered(1))
    mod_spec = pl.BlockSpec((1, 1, d), lambda bi, m: (bi, 0, 0))
    t_spec = pl.BlockSpec((1, d, tm), lambda bi, m: (bi, 0, m))
    return pl.pallas_call(
        functools.partial(_inproj_prompt_kernel, d=d, cols=cols),
        grid=(b, s // tm),
        in_specs=[pl.BlockSpec((1, tm, d), lambda bi, m: (bi, m, 0)), mod_spec, mod_spec,
                  resident(w.shape), resident(wt.shape)],
        out_specs=[t_spec, t_spec, pl.BlockSpec((1, tm, len(cols) * d), lambda bi, m: (bi, m, 0))],
        out_shape=[jax.ShapeDtypeStruct((b, d, s), F32), jax.ShapeDtypeStruct((b, d, s), F32),
                   jax.ShapeDtypeStruct((b, s, len(cols) * d), BF16)],
        compiler_params=_params("parallel", "parallel"),
        name="inproj_prompt",
    )(x, scale, shift, w, wt)


def _inproj_sample_kernel(x_ref, scale_ref, shift_ref, w_ref, wt_ref, t_ref, z_ref, h_ref, *, nt):
    j = pl.program_id(0)

    @pl.when(j == 0)
    def _():
        h_ref[...] = x_ref[...] * (1.0 + scale_ref[...]) + shift_ref[...]

    @pl.when(j < nt)
    def _():
        t_ref[0] = _dot3(wt_ref[0], h_ref[...], NT_DIMS)

    @pl.when(j >= nt)
    def _():
        z_ref[...] = _dot3(h_ref[...], w_ref[...])


def _inproj_sample(x, scale, shift, w, wt):
    r, d = x.shape
    ng, nt = w.shape[1] // d, wt.shape[0]
    rows = pl.BlockSpec((r, d), lambda j: (0, 0))
    return pl.pallas_call(
        functools.partial(_inproj_sample_kernel, nt=nt),
        grid=(ng,),
        in_specs=[rows, rows, rows,
                  pl.BlockSpec((d, d), lambda j: (0, jnp.maximum(j, nt))),
                  pl.BlockSpec((1, d, d), lambda j: (jnp.minimum(j, nt - 1), 0, 0))],
        out_specs=[pl.BlockSpec((1, d, r), lambda j: (jnp.minimum(j, nt - 1), 0, 0)),
                   pl.BlockSpec((r, d), lambda j: (0, jnp.maximum(j - nt, 0)))],
        out_shape=[jax.ShapeDtypeStruct((nt, d, r), F32),
                   jax.ShapeDtypeStruct((r, (ng - nt) * d), F32)],
        scratch_shapes=[pltpu.VMEM((r, d), F32)],
        compiler_params=_params("arbitrary"),
        name="inproj_sample",
    )(x, scale, shift, w, wt)


def _decode_step(seq, p, last, qt_ref, bias_ref, tri_ref, k_refs, v_refs, o_ref, qb_ref, acc_ref, run_ref, *, hd):
    nh, _, page = acc_ref.shape
    npg = len(k_refs)

    @pl.when(p == 0)
    def _():
        nb = qt_ref.shape[1]
        onehot = (lax.broadcasted_iota(jnp.int32, (nb, page), 0) == seq).astype(BF16)
        qb = _dot_hilo(qt_ref[...], onehot) * (hd ** -0.5 * LOG2E)
        qb_ref[...] = qb.reshape(qb_ref.shape)
        acc_ref[...] = jnp.zeros_like(acc_ref)
        run_ref[...] = jnp.zeros_like(run_ref)

    qb = qb_ref[...]
    bias2 = bias_ref[...] * LOG2E
    tri = tri_ref[...]
    run = run_ref[...]
    weights = []
    for kr in k_refs:
        y = jnp.sum(kr[0] * qb, axis=1) + bias2
        sp = _softplus2(y)
        ext = _dot_hilo(sp, tri)
        weights.append(jnp.exp2(y - sp - ext[:, :page] - run))
        run = run + ext[:, page:]
    run_ref[...] = run
    for h in range(nh):
        part = v_refs[0][0, h] * weights[0][h:h + 1, :]
        for i in range(1, npg):
            part = part + v_refs[i][0, h] * weights[i][h:h + 1, :]
        acc_ref[h] += part

    @pl.when(last)
    def _():
        o_ref[0] = jnp.sum(acc_ref[...], axis=-1)


def _attn_kernel(pt_ref, bias_ref, q_ref, kt_ref, vt_ref, tri_ref, qt_ref, dbias_ref, dtri_ref, *refs,
                 tq, hd, nh, npg, dsteps):
    k_pages, v_pages = refs[:npg], refs[npg:2 * npg]
    o_ref, od_ref, y_ref, incl_ref, acc_ref, run_ref, qb_ref, dacc_ref, drun_ref = refs[2 * npg:]
    g = pl.program_id(1)
    i = pl.program_id(2)
    n = (pl.program_id(0) * pl.num_programs(1) + g) * pl.num_programs(2) + i
    _decode_step(n // dsteps, n % dsteps, n % dsteps == dsteps - 1, qt_ref, dbias_ref, dtri_ref,
                 k_pages, v_pages, od_ref, qb_ref, dacc_ref, drun_ref, hd=hd)

    q = q_ref[0]
    tri = tri_ref[...]
    row = lax.broadcasted_iota(jnp.int32, (tq, tq), 0)
    col = lax.broadcasted_iota(jnp.int32, (tq, tq), 1)
    below_diag = col < row
    logit_scale = hd ** -0.5 * LOG2E

    def block_start(jj):
        return pl.multiple_of((i - jj) * tq, tq)

    def logits(jj, slot, diagonal=False):
        parts = []
        for hh in range(nh):
            rows = slice(hh * hd, (hh + 1) * hd)
            kt = kt_ref[0, rows, pl.ds(block_start(jj), tq)].astype(BF16)
            y = _dot(q[:, rows], kt) * logit_scale + bias_ref[g * nh + hh] * LOG2E
            if diagonal:
                y = jnp.where(below_diag, y, -1e30)
            y_ref[slot, hh * tq:(hh + 1) * tq] = y
            parts.append(jnp.concatenate(_split(_softplus2(y)), axis=1))
        incl_ref[slot] = _dot(jnp.concatenate(parts, axis=0), tri)

    def weights(jj, slot):
        for hh in range(nh):
            rows = slice(hh * hd, (hh + 1) * hd)
            vt = vt_ref[0, rows, pl.ds(block_start(jj), tq)].astype(BF16)
            incl = incl_ref[slot, hh * tq:(hh + 1) * tq]
            run = run_ref[hh]
            w = jnp.exp2(y_ref[slot, hh * tq:(hh + 1) * tq] - incl
                         - jnp.concatenate([run] * (tq // LANES), axis=1))
            acc_ref[hh] += _dot(w.astype(BF16), vt, NT_DIMS)
            run_ref[hh] = run + incl[:, :1]

    acc_ref[...] = jnp.zeros_like(acc_ref)
    run_ref[...] = jnp.zeros_like(run_ref)
    logits(0, 0, diagonal=True)

    @pl.loop(0, i // 2)
    def _(m):
        logits(2 * m + 1, 1)
        weights(2 * m, 0)
        logits(2 * m + 2, 0)
        weights(2 * m + 1, 1)

    @pl.when(i % 2 == 1)
    def _():
        logits(i, 1)
        weights(i - 1, 0)
        weights(i, 1)

    @pl.when(i % 2 == 0)
    def _():
        weights(i, 0)

    o_ref[0] = jnp.concatenate([acc_ref[hh] for hh in range(nh)], axis=-1).astype(o_ref.dtype)


def _attention(z, kt, vt, sb_bias, qt, cache_kt, cache_vt, page_table, *, d, hd, tq, nh):
    b, s, _ = z.shape
    width = nh * hd
    grid = (b, d // width, s // tq)
    r = lax.broadcasted_iota(jnp.int32, (2 * tq, tq), 0) % tq
    c = lax.broadcasted_iota(jnp.int32, (2 * tq, tq), 1)
    tri = (r >= c).astype(BF16)

    _, bs = qt.shape
    _, n_heads, _, page = cache_kt.shape
    n_pages = page_table.shape[1]
    steps = grid[0] * grid[1] * grid[2]
    assert steps % bs == 0 and n_pages % (steps // bs) == 0 and page == LANES
    dsteps = steps // bs
    npg = n_pages // dsteps
    r = lax.broadcasted_iota(jnp.int32, (page, 2 * page), 0)
    c = lax.broadcasted_iota(jnp.int32, (page, 2 * page), 1)
    dtri = jnp.logical_or(r > c, c >= page).astype(BF16)
    dbias = jnp.broadcast_to(sb_bias.astype(F32)[:, None], (n_heads, page))

    def step_of(bi, g, i):
        return (bi * grid[1] + g) * grid[2] + i

    def page_spec(k):
        def index(bi, g, i, pt):
            n = step_of(bi, g, i)
            return (pt[n // dsteps, n_pages - 1 - ((n % dsteps) * npg + k)], 0, 0, 0)
        return pl.BlockSpec((1, n_heads, hd, page), index)

    const = lambda bi, g, i, pt: (0, 0)
    grid_spec = pltpu.PrefetchScalarGridSpec(
        num_scalar_prefetch=1,
        grid=grid,
        in_specs=[pl.BlockSpec(memory_space=pltpu.SMEM),
                  pl.BlockSpec((1, tq, width), lambda bi, g, i, pt: (bi, i, g)),
                  pl.BlockSpec((1, width, s), lambda bi, g, i, pt: (bi, g, 0)),
                  pl.BlockSpec((1, width, s), lambda bi, g, i, pt: (bi, g, 0)),
                  pl.BlockSpec((2 * tq, tq), const),
                  pl.BlockSpec((d, bs), const), pl.BlockSpec((n_heads, page), const),
                  pl.BlockSpec((page, 2 * page), const)]
                 + [page_spec(k) for k in range(npg)] * 2,
        out_specs=[pl.BlockSpec((1, tq, width), lambda bi, g, i, pt: (bi, i, g)),
                   pl.BlockSpec((1, n_heads, hd), lambda bi, g, i, pt: (step_of(bi, g, i) // dsteps, 0, 0))],
        scratch_shapes=[pltpu.VMEM((2, nh * tq, tq), F32), pltpu.VMEM((2, nh * tq, tq), F32),
                        pltpu.VMEM((nh, tq, hd), F32), pltpu.VMEM((nh, tq, LANES), F32),
                        pltpu.VMEM((n_heads, hd, page), F32), pltpu.VMEM((n_heads, hd, page), F32),
                        pltpu.VMEM((n_heads, page), F32)],
    )
    return pl.pallas_call(
        functools.partial(_attn_kernel, tq=tq, hd=hd, nh=nh, npg=npg, dsteps=dsteps),
        grid_spec=grid_spec,
        out_shape=[jax.ShapeDtypeStruct((b, s, d), BF16), jax.ShapeDtypeStruct((bs, n_heads, hd), F32)],
        compiler_params=_params("arbitrary", "arbitrary", "arbitrary"),
        name="attn",
    )(page_table, sb_bias.astype(F32), z, kt, vt, tri, qt, dbias, dtri,
      *([cache_kt] * npg), *([cache_vt] * npg))


def _ssmprep_kernel(ar_ref, ai_ref, ldt_ref, br_ref, bi_ref, abr_ref, abi_ref, bbr_ref, bbi_ref):
    dt = jnp.exp(ldt_ref[...])
    lr, li = ar_ref[...], ai_ref[...]
    mag = jnp.exp(lr * dt)
    ab_re, ab_im = mag * jnp.cos(li * dt), mag * jnp.sin(li * dt)
    den = lr * lr + li * li
    nr = ab_re - 1.0
    co_re = (nr * lr + ab_im * li) / den
    co_im = (ab_im * lr - nr * li) / den
    abr_ref[...] = ab_re
    abi_ref[...] = ab_im
    br, bi = br_ref[...], bi_ref[...]
    bbr_ref[...] = co_re * br - co_im * bi
    bbi_ref[...] = co_re * bi + co_im * br


def _ssm_discretize(a_re, a_im, log_dt, b_re, b_im):
    g, p = a_re.shape
    c = b_re.shape[-1]
    rep = lambda a: jnp.repeat(a, c, axis=0)
    rows = lambda b: b.transpose(0, 2, 1).reshape(g * c, p)
    out = jax.ShapeDtypeStruct((g * c, p), F32)
    ab_re, ab_im, bb_re, bb_im = pl.pallas_call(
        _ssmprep_kernel, out_shape=[out] * 4, name="ssmprep",
    )(rep(a_re), rep(a_im), rep(log_dt.reshape(g, 1)), rows(b_re), rows(b_im))
    return ab_re[::c], ab_im[::c], bb_re.reshape(g, c, p), bb_im.reshape(g, c, p)


def _block_diag(w, gpb):
    g, r, k = w.shape
    w = w.reshape(g // gpb, gpb, r, k)
    eye = jnp.eye(gpb, dtype=w.dtype)
    return jnp.einsum("bgrk,gh->bgrhk", w, eye).reshape(g // gpb, gpb * r, gpb * k)


def _ssm_kernel(u_ref, h0r_ref, h0i_ref, bmat_ref, ar_ref, ai_ref, cmat_ref, d_ref,
                y_ref, xr_ref, xi_ref, bu_ref, sr_ref, si_ref, *, tt, nb, ns, precise):
    t = pl.program_id(1)

    @pl.when(t == 0)
    def _():
        sr_ref[...] = h0r_ref[...]
        si_ref[...] = h0i_ref[...]

    u = u_ref[...].reshape(tt * nb, u_ref.shape[-1])
    if precise:
        bu_ref[...] = _dot3(u, bmat_ref[0])
    else:
        bu_ref[...] = _dot(u.astype(BF16), bmat_ref[0].astype(BF16))
    ar = jnp.broadcast_to(ar_ref[0], (nb, ns))
    ai = jnp.broadcast_to(ai_ref[0], (nb, ns))

    def step(k, carry):
        xr, xi = carry
        rows = pl.ds(pl.multiple_of(k * nb, nb), nb)
        nr = ar * xr - ai * xi + bu_ref[rows, :ns]
        ni = ar * xi + ai * xr + bu_ref[rows, ns:]
        bu_ref[rows, :ns] = nr
        bu_ref[rows, ns:] = ni
        return nr, ni

    xr, xi = lax.fori_loop(0, tt, step, (sr_ref[...], si_ref[...]))
    sr_ref[...] = xr
    si_ref[...] = xi
    if precise:
        y = _dot3(bu_ref[...], cmat_ref[0])
    else:
        y = _dot(bu_ref[...].astype(BF16), cmat_ref[0].astype(BF16))
    y = y + d_ref[...] * u
    y_ref[...] = y.reshape(y_ref.shape).astype(y_ref.dtype)

    @pl.when(t == pl.num_programs(1) - 1)
    def _():
        xr_ref[...] = xr
        xi_ref[...] = xi


def _ssm_scan(u, h0_re, h0_im, bmat, ab_re, ab_im, cmat, d_skip, *, tt, precise):
    s, nb, d = u.shape
    nblk, ch, ns2 = bmat.shape
    ns = ns2 // 2
    assert s % tt == 0 and d == nblk * ch
    grid = (nblk, s // tt)
    state_spec = pl.BlockSpec((nb, ns), lambda g, t: (0, g))
    return pl.pallas_call(
        functools.partial(_ssm_kernel, tt=tt, nb=nb, ns=ns, precise=precise),
        grid=grid,
        in_specs=[pl.BlockSpec((tt, nb, ch), lambda g, t: (t, 0, g)),
                  state_spec, state_spec,
                  pl.BlockSpec((1, ch, ns2), lambda g, t: (g, 0, 0)),
                  pl.BlockSpec((1, 1, ns), lambda g, t: (g, 0, 0)),
                  pl.BlockSpec((1, 1, ns), lambda g, t: (g, 0, 0)),
                  pl.BlockSpec((1, ns2, ch), lambda g, t: (g, 0, 0)),
                  pl.BlockSpec((1, ch), lambda g, t: (0, g))],
        out_specs=[pl.BlockSpec((tt, nb, ch), lambda g, t: (t, 0, g)), state_spec, state_spec],
        out_shape=[jax.ShapeDtypeStruct((s, nb, d), F32),
                   jax.ShapeDtypeStruct(h0_re.shape, F32),
                   jax.ShapeDtypeStruct(h0_re.shape, F32)],
        scratch_shapes=[pltpu.VMEM((tt * nb, ns2), F32),
                        pltpu.VMEM((nb, ns), F32), pltpu.VMEM((nb, ns), F32)],
        compiler_params=_params("parallel", "arbitrary"),
        name="ssm",
    )(u, h0_re, h0_im, bmat, ab_re, ab_im, cmat, d_skip)


def _ssm_prompt_kernel(u_ref, bmat_ref, ar_ref, ai_ref, cmat_ref, d_ref, y_ref, xr_ref, xi_ref,
                       bu_ref, sr_ref, si_ref, *, tt, nb, pitch, chunk):
    t = pl.program_id(1)

    @pl.when(t == 0)
    def _():
        sr_ref[...] = jnp.zeros_like(sr_ref)
        si_ref[...] = jnp.zeros_like(si_ref)

    nct = bu_ref.shape[0]
    half = nct // 2
    bmat = bmat_ref[0].astype(BF16)
    cmat = cmat_ref[0].astype(BF16)
    lanes = lambda a, c: a[:, c * LANES:(c + 1) * LANES]
    ar = [jnp.broadcast_to(lanes(ar_ref[0], c), (nb, LANES)) for c in range(half)]
    ai = [jnp.broadcast_to(lanes(ai_ref[0], c), (nb, LANES)) for c in range(half)]
    seq_rows = lambda b, c: slice(b * pitch + c * chunk, b * pitch + (c + 1) * chunk)

    def inputs(c):
        return jnp.concatenate([u_ref[b, c * chunk:(c + 1) * chunk, :] for b in range(nb)], axis=0)

    def fill(c):
        bu = _dot(inputs(c), bmat)
        for b in range(nb):
            for ct in range(nct):
                bu_ref[ct, seq_rows(b, c), :] = bu[b * chunk:(b + 1) * chunk, ct * LANES:(ct + 1) * LANES]

    def scan(c, xr, xi):
        for k in range(c * chunk, (c + 1) * chunk):
            rows = pl.ds(k, nb, stride=pitch)
            for ct in range(half):
                r = ar[ct] * xr[ct] - ai[ct] * xi[ct] + bu_ref[ct, rows, :]
                i = ar[ct] * xi[ct] + ai[ct] * xr[ct] + bu_ref[half + ct, rows, :]
                bu_ref[ct, rows, :] = r
                bu_ref[half + ct, rows, :] = i
                xr[ct], xi[ct] = r, i
        return xr, xi

    def readout(c):
        x = jnp.concatenate([jnp.concatenate([bu_ref[ct, seq_rows(b, c), :] for ct in range(nct)], axis=1)
                             for b in range(nb)], axis=0)
        y = _dot(x.astype(BF16), cmat) + d_ref[...] * inputs(c).astype(F32)
        for b in range(nb):
            y_ref[b, c * chunk:(c + 1) * chunk, :] = y[b * chunk:(b + 1) * chunk].astype(y_ref.dtype)

    xr = [lanes(sr_ref[...], c) for c in range(half)]
    xi = [lanes(si_ref[...], c) for c in range(half)]
    nchunks = tt // chunk
    fill(0)
    for c in range(nchunks):
        if c + 1 < nchunks:
            fill(c + 1)
        xr, xi = scan(c, xr, xi)
        if c >= 1:
            readout(c - 1)
    readout(nchunks - 1)
    xr, xi = jnp.concatenate(xr, axis=1), jnp.concatenate(xi, axis=1)
    sr_ref[...] = xr
    si_ref[...] = xi

    @pl.when(t == pl.num_programs(1) - 1)
    def _():
        xr_ref[...] = xr
        xi_ref[...] = xi


def _ssm_prompt(z, ucol, bmat, ab_re, ab_im, cmat, d_skip, *, tt):
    nb, s, _ = z.shape
    nblk, ch, ns2 = bmat.shape
    ns = ns2 // 2
    d = nblk * ch
    assert s % tt == 0 and nb % 8 == 0
    pitch = tt + 8
    state_spec = pl.BlockSpec((nb, ns), lambda g, t: (0, g))
    state_shape = jax.ShapeDtypeStruct((nb, nblk * ns), F32)
    return pl.pallas_call(
        functools.partial(_ssm_prompt_kernel, tt=tt, nb=nb, pitch=pitch, chunk=_pick(tt, 32)),
        grid=(nblk, s // tt),
        in_specs=[pl.BlockSpec((nb, tt, ch), lambda g, t: (0, t, ucol * nblk + g)),
                  pl.BlockSpec((1, ch, ns2), lambda g, t: (g, 0, 0)),
                  pl.BlockSpec((1, 1, ns), lambda g, t: (g, 0, 0)),
                  pl.BlockSpec((1, 1, ns), lambda g, t: (g, 0, 0)),
                  pl.BlockSpec((1, ns2, ch), lambda g, t: (g, 0, 0)),
                  pl.BlockSpec((1, ch), lambda g, t: (0, g))],
        out_specs=[pl.BlockSpec((nb, tt, ch), lambda g, t: (0, t, g)), state_spec, state_spec],
        out_shape=[jax.ShapeDtypeStruct((nb, s, d), BF16), state_shape, state_shape],
        scratch_shapes=[pltpu.VMEM((ns2 // LANES, nb * pitch, LANES), F32),
                        pltpu.VMEM((nb, ns), F32), pltpu.VMEM((nb, ns), F32)],
        compiler_params=_params("parallel", "arbitrary"),
        name="ssm_prompt",
    )(z, bmat, ab_re, ab_im, cmat, d_skip)


def _post_kernel(x_ref, gate_ref, y_ref, o_ref, gatt_ref, gssm_ref, matt_ref, mssm_ref,
                 wglu_ref, bglu_ref, watt_ref, wssm_ref, wout_ref, lng_ref, lnb_ref, out_ref, *, alpha):
    g = jax.nn.gelu(y_ref[0].astype(F32))
    glu = g * jax.nn.sigmoid(_dot(g.astype(BF16), wglu_ref[...]) + bglu_ref[...])
    ys = glu * jax.nn.silu(gssm_ref[0].astype(F32))
    y_ssm = _dot(ys.astype(BF16), wssm_ref[...])
    ya = o_ref[0].astype(F32) * jax.nn.silu(gatt_ref[0].astype(F32))
    y_att = _dot(ya.astype(BF16), watt_ref[...])
    merged = (jax.nn.sigmoid(matt_ref[0].astype(F32)) * y_att
              + jax.nn.sigmoid(mssm_ref[0].astype(F32)) * y_ssm)
    r = alpha * x_ref[0] + gate_ref[0] * _dot(merged.astype(BF16), wout_ref[...])
    mu = jnp.mean(r, axis=-1, keepdims=True)
    cen = r - mu
    var = jnp.mean(cen * cen, axis=-1, keepdims=True)
    out_ref[0] = cen * lax.rsqrt(var + LN_EPS) * lng_ref[...] + lnb_ref[...]


def _post(x, gate, y, o_att, z, cols, w_glu, b_glu, w_att_out, w_ssm_out, w_out, ln_g, ln_b, *, tm, alpha):
    b, s, d = x.shape
    per_row = gate.shape[1] != 1
    row = lambda col: pl.BlockSpec((1, tm, d), lambda bi, m, col=col: (bi, m, col))
    gate_spec = row(0) if per_row else pl.BlockSpec((1, 1, d), lambda bi, m: (bi, 0, 0))
    mat = pl.BlockSpec((d, d), lambda bi, m: (0, 0))
    vec = pl.BlockSpec((1, d), lambda bi, m: (0, 0))
    return pl.pallas_call(
        functools.partial(_post_kernel, alpha=alpha),
        grid=(b, s // tm),
        in_specs=[row(0), gate_spec, row(0), row(0)] + [row(c) for c in cols]
                 + [mat, vec, mat, mat, mat, vec, vec],
        out_specs=row(0),
        out_shape=jax.ShapeDtypeStruct((b, s, d), F32),
        compiler_params=_params("parallel", "parallel"),
        name="post",
    )(x, gate, y, o_att, z, z, z, z,
      w_glu.astype(BF16), b_glu.reshape(1, d), w_att_out.astype(BF16), w_ssm_out.astype(BF16),
      w_out.astype(BF16), ln_g.reshape(1, d), ln_b.reshape(1, d))


def _pick(n, target):
    t = min(n, target)
    while n % t:
        t //= 2
    return t


def kernel(x_prompt, x_sample, c_prompt, c_sample, cache_k, cache_v, state_ssm_re, state_ssm_im, page_table, w_cond, b_cond, w_in, sb_bias, ssm_a_re, ssm_a_im, ssm_log_dt, ssm_b_re, ssm_b_im, ssm_c_re, ssm_c_im, ssm_d, w_glu, b_glu, w_att_out, w_ssm_out, w_out, ln_g, ln_b):
    depth = w_in.shape[0]
    assert depth == 1, "single-layer trunk"
    bsz, seq, d = x_prompt.shape
    dbsz, dseq, _ = x_sample.shape
    assert dseq == 1 and w_in.shape[2] == 8 * d
    n_heads = sb_bias.shape[-1]
    hd = cache_k.shape[-1]
    n_groups, n_state = ssm_a_re.shape[1:]
    alpha = (2.0 * depth) ** 0.25
    l = 0

    mod = _modulation(jnp.concatenate([c_prompt, c_sample], axis=0), w_cond[l], b_cond[l])
    shift, scale, gate = mod[:, :d], mod[:, d:2 * d], mod[:, 2 * d:]
    row3 = lambda a, lo, hi, shape: a[lo:hi].reshape(shape)
    p_shape, s_shape = (bsz, 1, d), (1, dbsz, d)

    ab_re, ab_im, bb_re, bb_im = _ssm_discretize(ssm_a_re[l], ssm_a_im[l], ssm_log_dt[l],
                                                 ssm_b_re[l], ssm_b_im[l])
    gpb = GROUPS_PER_BLOCK
    nblk = n_groups // gpb
    bmat = jnp.concatenate([_block_diag(bb_re, gpb), _block_diag(bb_im, gpb)], axis=-1)
    cmat = jnp.concatenate([_block_diag(ssm_c_re[l].transpose(0, 2, 1), gpb),
                            _block_diag(-ssm_c_im[l].transpose(0, 2, 1), gpb)], axis=1)
    a_blk = lambda a: a.reshape(nblk, 1, gpb * n_state)
    d_skip = ssm_d[l].reshape(1, d)

    w = w_in[l]
    wt = w[:, :3 * d].reshape(d, 3, d).transpose(1, 2, 0)
    wt_kvq = jnp.stack([wt[1], wt[2], wt[0]])

    kt_p, vt_p, z_p = _inproj_prompt(x_prompt, row3(scale, 0, bsz, p_shape), row3(shift, 0, bsz, p_shape),
                                     w.astype(BF16), wt_kvq[:2].astype(BF16),
                                     tm=_pick(seq, 512), cols=(0, 3, 4, 5, 6, 7))
    x_s = x_sample.reshape(dbsz, d)
    kvqt_s, z_s = _inproj_sample(x_s, scale[bsz:], shift[bsz:], w, wt_kvq)
    z_s = z_s.reshape(1, dbsz, 5 * d)

    pool = lambda c: c[l].transpose(0, 2, 3, 1)
    o_p, o_s = _attention(z_p, kt_p, vt_p, sb_bias[l], kvqt_s[2], pool(cache_k), pool(cache_v), page_table,
                          d=d, hd=hd, tq=_pick(seq, 256), nh=4)

    y_p, sr_p, si_p = _ssm_prompt(z_p, 2, bmat, a_blk(ab_re), a_blk(ab_im), cmat, d_skip, tt=_pick(seq, 256))
    out_p = _post(x_prompt, row3(gate, 0, bsz, p_shape), y_p, o_p, z_p, (1, 3, 4, 5),
                  w_glu[l], b_glu[l], w_att_out[l], w_ssm_out[l], w_out[l], ln_g[l], ln_b[l],
                  tm=_pick(seq, 512), alpha=alpha)
    y_s, sr_s, si_s = _ssm_scan(z_s[:, :, d:2 * d], state_ssm_re[l].reshape(dbsz, -1),
                                state_ssm_im[l].reshape(dbsz, -1), bmat, a_blk(ab_re), a_blk(ab_im),
                                cmat, d_skip, tt=1, precise=True)
    out_s = _post(x_s.reshape(1, dbsz, d), row3(gate, bsz, bsz + dbsz, s_shape), y_s, o_s.reshape(1, dbsz, d),
                  z_s, (0, 2, 3, 4),
                  w_glu[l], b_glu[l], w_att_out[l], w_ssm_out[l], w_out[l], ln_g[l], ln_b[l],
                  tm=dbsz, alpha=alpha)

    heads_t = lambda t, b, s: t.reshape(1, b, n_heads, hd, s).transpose(0, 1, 4, 2, 3)
    state = lambda t, b: t.reshape(1, b, n_groups, n_state)
    return (out_p, out_s.reshape(dbsz, 1, d),
            heads_t(kt_p, bsz, seq), heads_t(vt_p, bsz, seq), state(sr_p, bsz), state(si_p, bsz),
            heads_t(kvqt_s[0], 1, dbsz).reshape(1, dbsz, 1, n_heads, hd),
            heads_t(kvqt_s[1], 1, dbsz).reshape(1, dbsz, 1, n_heads, hd),
            state(sr_s, dbsz), state(si_s, dbsz))
```

```python
import functools
import math

import jax
import jax.numpy as jnp
from jax import lax
from jax.experimental import pallas as pl
from jax.experimental.pallas import tpu as pltpu

F32 = jnp.float32
BF16 = jnp.bfloat16
LN_EPS = 1e-5
LOG2E = math.log2(math.e)
LANES = 128
GROUPS_PER_BLOCK = 8
VMEM_LIMIT = 56 * 1024 * 1024
NT_DIMS = (((1,), (1,)), ((), ()))


def _dot(a, b, dims=None):
    if dims is None:
        return jnp.dot(a, b, preferred_element_type=F32)
    return lax.dot_general(a, b, dims, preferred_element_type=F32)


def _split(a):
    hi = a.astype(BF16)
    lo = (a - hi.astype(F32)).astype(BF16)
    return hi, lo


def _dot3(a, b, dims=None):
    ah, al = _split(a)
    bh, bl = _split(b)
    return _dot(ah, bh, dims) + (_dot(ah, bl, dims) + _dot(al, bh, dims))


def _dot_hilo(a, b_bf16):
    hi, lo = _split(a)
    return _dot(hi, b_bf16) + _dot(lo, b_bf16)


def _softplus2(y):
    return jnp.maximum(y, 0.0) + jnp.log2(1.0 + jnp.exp2(-jnp.abs(y)))


def _params(*sem):
    return pltpu.CompilerParams(dimension_semantics=sem, vmem_limit_bytes=VMEM_LIMIT)


def _mod_kernel(c_ref, w_ref, b_ref, o_ref):
    o_ref[...] = _dot3(c_ref[...], w_ref[...]) + b_ref[...]


def _modulation(c, w_cond, b_cond):
    rows, d = c.shape
    n = w_cond.shape[1]
    tn = 768 if n % 768 == 0 else n
    return pl.pallas_call(
        _mod_kernel,
        grid=(n // tn,),
        in_specs=[pl.BlockSpec((rows, d), lambda j: (0, 0)),
                  pl.BlockSpec((d, tn), lambda j: (0, j)),
                  pl.BlockSpec((1, tn), lambda j: (0, j))],
        out_specs=pl.BlockSpec((rows, tn), lambda j: (0, j)),
        out_shape=jax.ShapeDtypeStruct((rows, n), F32),
        compiler_params=_params("arbitrary"),
        name="mod",
    )(c, w_cond, b_cond.reshape(1, n))


def _inproj_prompt_kernel(x_ref, scale_ref, shift_ref, w_ref, wt_ref, kt_ref, vt_ref, z_ref, *, d, cols):
    h = (x_ref[0] * (1.0 + scale_ref[0]) + shift_ref[0]).astype(BF16)
    kt_ref[0] = _dot(wt_ref[0], h, NT_DIMS)
    vt_ref[0] = _dot(wt_ref[1], h, NT_DIMS)
    for gi, c in enumerate(cols):
        z_ref[0, :, gi * d:(gi + 1) * d] = _dot(h, w_ref[:, c * d:(c + 1) * d]).astype(z_ref.dtype)


def _inproj_prompt(x, scale, shift, w, wt, *, tm, cols):
    b, s, d = x.shape
    assert s % tm == 0
    resident = lambda shape: pl.BlockSpec(shape, lambda bi, m: (0,) * len(shape),
                                          pipeline_mode=pl.Buffered(1))
    mod_spec = pl.BlockSpec((1, 1, d), lambda bi, m: (bi, 0, 0))
    t_spec = pl.BlockSpec((1, d, tm), lambda bi, m: (bi, 0, m))
    return pl.pallas_call(
        functools.partial(_inproj_prompt_kernel, d=d, cols=cols),
        grid=(b, s // tm),
        in_specs=[pl.BlockSpec((1, tm, d), lambda bi, m: (bi, m, 0)), mod_spec, mod_spec,
                  resident(w.shape), resident(wt.shape)],
        out_specs=[t_spec, t_spec, pl.BlockSpec((1, tm, len(cols) * d), lambda bi, m: (bi, m, 0))],
        out_shape=[jax.ShapeDtypeStruct((b, d, s), F32), jax.ShapeDtypeStruct((b, d, s), F32),
                   jax.ShapeDtypeStruct((b, s, len(cols) * d), BF16)],
        compiler_params=_params("parallel", "parallel"),
        name="inproj_prompt",
    )(x, scale, shift, w, wt)


def _inproj_sample_kernel(x_ref, scale_ref, shift_ref, w_ref, wt_ref, t_ref, z_ref, h_ref, *, nt):
    j = pl.program_id(0)

    @pl.when(j == 0)
    def _():
        h_ref[...] = x_ref[...] * (1.0 + scale_ref[...]) + shift_ref[...]

    @pl.when(j < nt)
    def _():
        t_ref[0] = _dot3(wt_ref[0], h_ref[...], NT_DIMS)

    @pl.when(j >= nt)
    def _():
        z_ref[...] = _dot3(h_ref[...], w_ref[...])


def _inproj_sample(x, scale, shift, w, wt):
    r, d = x.shape
    ng, nt = w.shape[1] // d, wt.shape[0]
    rows = pl.BlockSpec((r, d), lambda j: (0, 0))
    return pl.pallas_call(
        functools.partial(_inproj_sample_kernel, nt=nt),
        grid=(ng,),
        in_specs=[rows, rows, rows,
                  pl.BlockSpec((d, d), lambda j: (0, jnp.maximum(j, nt))),
                  pl.BlockSpec((1, d, d), lambda j: (jnp.minimum(j, nt - 1), 0, 0))],
        out_specs=[pl.BlockSpec((1, d, r), lambda j: (jnp.minimum(j, nt - 1), 0, 0)),
                   pl.BlockSpec((r, d), lambda j: (0, jnp.maximum(j - nt, 0)))],
        out_shape=[jax.ShapeDtypeStruct((nt, d, r), F32),
                   jax.ShapeDtypeStruct((r, (ng - nt) * d), F32)],
        scratch_shapes=[pltpu.VMEM((r, d), F32)],
        compiler_params=_params("arbitrary"),
        name="inproj_sample",
    )(x, scale, shift, w, wt)


def _decode_begin(seq, qt_ref, qb_ref, acc_ref, run_ref, *, hd):
    nb, page = qt_ref.shape[1], acc_ref.shape[-1]
    onehot = (lax.broadcasted_iota(jnp.int32, (nb, page), 0) == seq).astype(BF16)
    qb = _dot_hilo(qt_ref[...], onehot) * (hd ** -0.5 * LOG2E)
    qb_ref[...] = qb.reshape(qb_ref.shape)
    acc_ref[...] = jnp.zeros_like(acc_ref)
    run_ref[...] = jnp.zeros_like(run_ref)


def _decode_step(bias_ref, tri_ref, k_refs, v_refs, qb_ref, acc_ref, run_ref):
    nh, _, page = acc_ref.shape
    npg = len(k_refs)
    qb = qb_ref[...]
    bias2 = bias_ref[...] * LOG2E
    tri = tri_ref[...]
    run = run_ref[...]
    weights = []
    for kr in k_refs:
        y = jnp.sum(kr[0] * qb, axis=1) + bias2
        sp = _softplus2(y)
        ext = _dot_hilo(sp, tri)
        weights.append(jnp.exp2(y - sp - ext[:, :page] - run))
        run = run + ext[:, page:]
    run_ref[...] = run
    for h in range(nh):
        part = v_refs[0][0, h] * weights[0][h:h + 1, :]
        for i in range(1, npg):
            part = part + v_refs[i][0, h] * weights[i][h:h + 1, :]
        acc_ref[h] += part


def _attn_kernel(pt_ref, bias_ref, q_ref, kt_ref, vt_ref, tri_ref, qt_ref, dbias_ref, dtri_ref, *refs,
                 tq, hd, nh, npg, dsteps):
    k_pages, v_pages = refs[:npg], refs[npg:2 * npg]
    (o_ref, od_ref, y_ref, incl_ref, acc_ref, run_ref, ktb_ref, vtb_ref,
     qb_ref, dacc_ref, drun_ref) = refs[2 * npg:]
    g = pl.program_id(1)
    i = pl.program_id(2)
    n = (pl.program_id(0) * pl.num_programs(1) + g) * pl.num_programs(2) + i
    assert 2 * hd == LANES and nh % 2 == 0

    @pl.when(n % dsteps == 0)
    def _():
        _decode_begin(n // dsteps, qt_ref, qb_ref, dacc_ref, drun_ref, hd=hd)

    @pl.when(i == 0)
    def _():
        s = kt_ref.shape[-1]
        r = lax.broadcasted_iota(jnp.int32, (hd, s), 0)
        for hh in range(nh):
            rows = slice(hh * hd, (hh + 1) * hd)
            keys = (kt_ref[0, rows, :] * (hd ** -0.5 * LOG2E)).astype(BF16)
            bias2 = jnp.full((hd, s), bias_ref[g * nh + hh] * LOG2E, F32)
            b_hi = bias2.astype(BF16).astype(F32)
            bias_rows = jnp.where(r == 0, b_hi, jnp.where(r == 1, bias2 - b_hi, 0.0)).astype(BF16)
            pair = [keys, bias_rows] if hh % 2 == 0 else [bias_rows, keys]
            ktb_ref[hh * LANES:(hh + 1) * LANES, :] = jnp.concatenate(pair, axis=0)
        vtb_ref[...] = vt_ref[0].astype(BF16)

    tri = tri_ref[...]
    row = lax.broadcasted_iota(jnp.int32, (tq, tq), 0)
    col = lax.broadcasted_iota(jnp.int32, (tq, tq), 1)
    below_diag = col < row
    lane = lax.broadcasted_iota(jnp.int32, (tq, LANES), 1)
    q_ext = []
    for hh in range(nh):
        tile = q_ref[0, :, (hh // 2) * LANES:(hh // 2 + 1) * LANES].astype(F32)
        q_ext.append(jnp.where(lane < hd if hh % 2 == 0 else lane >= hd, tile, 1.0).astype(BF16))

    def block_start(jj):
        return pl.multiple_of((i - jj) * tq, tq)

    def logits(jj, slot, diagonal=False):
        parts = []
        for hh in range(nh):
            y = _dot(q_ext[hh], ktb_ref[hh * LANES:(hh + 1) * LANES, pl.ds(block_start(jj), tq)])
            if diagonal:
                y = jnp.where(below_diag, y, -1e30)
            y_ref[slot, hh * tq:(hh + 1) * tq] = y
            parts.append(jnp.concatenate(_split(_softplus2(y)), axis=1))
        incl_ref[slot] = _dot(jnp.concatenate(parts, axis=0), tri)

    def weights(jj, slot):
        for hh in range(nh):
            vt = vtb_ref[hh * hd:(hh + 1) * hd, pl.ds(block_start(jj), tq)]
            incl = incl_ref[slot, hh * tq:(hh + 1) * tq]
            run = run_ref[hh]
            w = jnp.exp2(y_ref[slot, hh * tq:(hh + 1) * tq] - incl
                         - jnp.concatenate([run] * (tq // LANES), axis=1))
            acc_ref[hh] += _dot(w.astype(BF16), vt, NT_DIMS)
            run_ref[hh] = run + incl[:, :1]

    _decode_step(dbias_ref, dtri_ref, k_pages, v_pages, qb_ref, dacc_ref, drun_ref)
    acc_ref[...] = jnp.zeros_like(acc_ref)
    run_ref[...] = jnp.zeros_like(run_ref)
    logits(0, 0, diagonal=True)

    @pl.loop(0, i // 2)
    def _(m):
        logits(2 * m + 1, 1)
        weights(2 * m, 0)
        logits(2 * m + 2, 0)
        weights(2 * m + 1, 1)

    @pl.when(i % 2 == 1)
    def _():
        logits(i, 1)
        weights(i - 1, 0)
        weights(i, 1)

    @pl.when(i % 2 == 0)
    def _():
        weights(i, 0)

    o_ref[0] = jnp.concatenate([acc_ref[hh] for hh in range(nh)], axis=-1).astype(o_ref.dtype)

    @pl.when(n % dsteps == dsteps - 1)
    def _():
        od_ref[0] = jnp.sum(dacc_ref[...], axis=-1)


def _attention(z, kt, vt, sb_bias, qt, cache_kt, cache_vt, page_table, *, d, hd, tq, nh):
    b, s, _ = z.shape
    width = nh * hd
    grid = (b, d // width, s // tq)
    r = lax.broadcasted_iota(jnp.int32, (2 * tq, tq), 0) % tq
    c = lax.broadcasted_iota(jnp.int32, (2 * tq, tq), 1)
    tri = (r >= c).astype(BF16)

    _, bs = qt.shape
    _, n_heads, _, page = cache_kt.shape
    n_pages = page_table.shape[1]
    steps = grid[0] * grid[1] * grid[2]
    assert steps % bs == 0 and n_pages % (steps // bs) == 0 and page == LANES
    dsteps = steps // bs
    npg = n_pages // dsteps
    r = lax.broadcasted_iota(jnp.int32, (page, 2 * page), 0)
    c = lax.broadcasted_iota(jnp.int32, (page, 2 * page), 1)
    dtri = jnp.logical_or(r > c, c >= page).astype(BF16)
    dbias = jnp.broadcast_to(sb_bias.astype(F32)[:, None], (n_heads, page))

    def step_of(bi, g, i):
        return (bi * grid[1] + g) * grid[2] + i

    def page_spec(k):
        def index(bi, g, i, pt):
            n = step_of(bi, g, i)
            return (pt[n // dsteps, n_pages - 1 - ((n % dsteps) * npg + k)], 0, 0, 0)
        return pl.BlockSpec((1, n_heads, hd, page), index)

    const = lambda bi, g, i, pt: (0, 0)
    grid_spec = pltpu.PrefetchScalarGridSpec(
        num_scalar_prefetch=1,
        grid=grid,
        in_specs=[pl.BlockSpec(memory_space=pltpu.SMEM),
                  pl.BlockSpec((1, tq, width), lambda bi, g, i, pt: (bi, i, g)),
                  pl.BlockSpec((1, width, s), lambda bi, g, i, pt: (bi, g, 0)),
                  pl.BlockSpec((1, width, s), lambda bi, g, i, pt: (bi, g, 0)),
                  pl.BlockSpec((2 * tq, tq), const),
                  pl.BlockSpec((d, bs), const), pl.BlockSpec((n_heads, page), const),
                  pl.BlockSpec((page, 2 * page), const)]
                 + [page_spec(k) for k in range(npg)] * 2,
        out_specs=[pl.BlockSpec((1, tq, width), lambda bi, g, i, pt: (bi, i, g)),
                   pl.BlockSpec((1, n_heads, hd), lambda bi, g, i, pt: (step_of(bi, g, i) // dsteps, 0, 0))],
        scratch_shapes=[pltpu.VMEM((2, nh * tq, tq), F32), pltpu.VMEM((2, nh * tq, tq), F32),
                        pltpu.VMEM((nh, tq, hd), F32), pltpu.VMEM((nh, tq, LANES), F32),
                        pltpu.VMEM((nh * LANES, s), BF16), pltpu.VMEM((width, s), BF16),
                        pltpu.VMEM((n_heads, hd, page), F32), pltpu.VMEM((n_heads, hd, page), F32),
                        pltpu.VMEM((n_heads, page), F32)],
    )
    return pl.pallas_call(
        functools.partial(_attn_kernel, tq=tq, hd=hd, nh=nh, npg=npg, dsteps=dsteps),
        grid_spec=grid_spec,
        out_shape=[jax.ShapeDtypeStruct((b, s, d), BF16), jax.ShapeDtypeStruct((bs, n_heads, hd), F32)],
        compiler_params=_params("arbitrary", "arbitrary", "arbitrary"),
        name="attn",
    )(page_table, sb_bias.astype(F32), z, kt, vt, tri, qt, dbias, dtri,
      *([cache_kt] * npg), *([cache_vt] * npg))


def _ssmprep_kernel(ar_ref, ai_ref, ldt_ref, br_ref, bi_ref, abr_ref, abi_ref, bbr_ref, bbi_ref):
    dt = jnp.exp(ldt_ref[...])
    lr, li = ar_ref[...], ai_ref[...]
    mag = jnp.exp(lr * dt)
    ab_re, ab_im = mag * jnp.cos(li * dt), mag * jnp.sin(li * dt)
    den = lr * lr + li * li
    nr = ab_re - 1.0
    co_re = (nr * lr + ab_im * li) / den
    co_im = (ab_im * lr - nr * li) / den
    abr_ref[...] = ab_re
    abi_ref[...] = ab_im
    br, bi = br_ref[...], bi_ref[...]
    bbr_ref[...] = co_re * br - co_im * bi
    bbi_ref[...] = co_re * bi + co_im * br


def _ssm_discretize(a_re, a_im, log_dt, b_re, b_im):
    g, p = a_re.shape
    c = b_re.shape[-1]
    rep = lambda a: jnp.repeat(a, c, axis=0)
    rows = lambda b: b.transpose(0, 2, 1).reshape(g * c, p)
    out = jax.ShapeDtypeStruct((g * c, p), F32)
    ab_re, ab_im, bb_re, bb_im = pl.pallas_call(
        _ssmprep_kernel, out_shape=[out] * 4, name="ssmprep",
    )(rep(a_re), rep(a_im), rep(log_dt.reshape(g, 1)), rows(b_re), rows(b_im))
    return ab_re[::c], ab_im[::c], bb_re.reshape(g, c, p), bb_im.reshape(g, c, p)


def _block_diag(w, gpb):
    g, r, k = w.shape
    w = w.reshape(g // gpb, gpb, r, k)
    eye = jnp.eye(gpb, dtype=w.dtype)
    return jnp.einsum("bgrk,gh->bgrhk", w, eye).reshape(g // gpb, gpb * r, gpb * k)


def _ssm_kernel(u_ref, h0r_ref, h0i_ref, bmat_ref, ar_ref, ai_ref, cmat_ref, d_ref,
                y_ref, xr_ref, xi_ref, bu_ref, sr_ref, si_ref, *, tt, nb, ns, precise):
    t = pl.program_id(1)

    @pl.when(t == 0)
    def _():
        sr_ref[...] = h0r_ref[...]
        si_ref[...] = h0i_ref[...]

    u = u_ref[...].reshape(tt * nb, u_ref.shape[-1])
    if precise:
        bu_ref[...] = _dot3(u, bmat_ref[0])
    else:
        bu_ref[...] = _dot(u.astype(BF16), bmat_ref[0].astype(BF16))
    ar = jnp.broadcast_to(ar_ref[0], (nb, ns))
    ai = jnp.broadcast_to(ai_ref[0], (nb, ns))

    def step(k, carry):
        xr, xi = carry
        rows = pl.ds(pl.multiple_of(k * nb, nb), nb)
        nr = ar * xr - ai * xi + bu_ref[rows, :ns]
        ni = ar * xi + ai * xr + bu_ref[rows, ns:]
        bu_ref[rows, :ns] = nr
        bu_ref[rows, ns:] = ni
        return nr, ni

    xr, xi = lax.fori_loop(0, tt, step, (sr_ref[...], si_ref[...]))
    sr_ref[...] = xr
    si_ref[...] = xi
    if precise:
        y = _dot3(bu_ref[...], cmat_ref[0])
    else:
        y = _dot(bu_ref[...].astype(BF16), cmat_ref[0].astype(BF16))
    y = y + d_ref[...] * u
    y_ref[...] = y.reshape(y_ref.shape).astype(y_ref.dtype)

    @pl.when(t == pl.num_programs(1) - 1)
    def _():
        xr_ref[...] = xr
        xi_ref[...] = xi


def _ssm_scan(u, h0_re, h0_im, bmat, ab_re, ab_im, cmat, d_skip, *, tt, precise):
    s, nb, d = u.shape
    nblk, ch, ns2 = bmat.shape
    ns = ns2 // 2
    assert s % tt == 0 and d == nblk * ch
    grid = (nblk, s // tt)
    state_spec = pl.BlockSpec((nb, ns), lambda g, t: (0, g))
    return pl.pallas_call(
        functools.partial(_ssm_kernel, tt=tt, nb=nb, ns=ns, precise=precise),
        grid=grid,
        in_specs=[pl.BlockSpec((tt, nb, ch), lambda g, t: (t, 0, g)),
                  state_spec, state_spec,
                  pl.BlockSpec((1, ch, ns2), lambda g, t: (g, 0, 0)),
                  pl.BlockSpec((1, 1, ns), lambda g, t: (g, 0, 0)),
                  pl.BlockSpec((1, 1, ns), lambda g, t: (g, 0, 0)),
                  pl.BlockSpec((1, ns2, ch), lambda g, t: (g, 0, 0)),
                  pl.BlockSpec((1, ch), lambda g, t: (0, g))],
        out_specs=[pl.BlockSpec((tt, nb, ch), lambda g, t: (t, 0, g)), state_spec, state_spec],
        out_shape=[jax.ShapeDtypeStruct((s, nb, d), F32),
                   jax.ShapeDtypeStruct(h0_re.shape, F32),
                   jax.ShapeDtypeStruct(h0_re.shape, F32)],
        scratch_shapes=[pltpu.VMEM((tt * nb, ns2), F32),
                        pltpu.VMEM((nb, ns), F32), pltpu.VMEM((nb, ns), F32)],
        compiler_params=_params("parallel", "arbitrary"),
        name="ssm",
    )(u, h0_re, h0_im, bmat, ab_re, ab_im, cmat, d_skip)


def _ssm_prompt_kernel(u_ref, bmat_ref, ar_ref, ai_ref, cmat_ref, d_ref, y_ref, xr_ref, xi_ref,
                       bu_ref, sr_ref, si_ref, *, tt, nb, pitch, chunk):
    t = pl.program_id(1)

    @pl.when(t == 0)
    def _():
        sr_ref[...] = jnp.zeros_like(sr_ref)
        si_ref[...] = jnp.zeros_like(si_ref)

    nct = bu_ref.shape[0]
    half = nct // 2
    bmat = bmat_ref[0].astype(BF16)
    cmat = cmat_ref[0].astype(BF16)
    lanes = lambda a, c: a[:, c * LANES:(c + 1) * LANES]
    ar = [jnp.broadcast_to(lanes(ar_ref[0], c), (nb, LANES)) for c in range(half)]
    ai = [jnp.broadcast_to(lanes(ai_ref[0], c), (nb, LANES)) for c in range(half)]
    seq_rows = lambda b, c: slice(b * pitch + c * chunk, b * pitch + (c + 1) * chunk)

    def inputs(c):
        return jnp.concatenate([u_ref[b, c * chunk:(c + 1) * chunk, :] for b in range(nb)], axis=0)

    def fill(c):
        bu = _dot(inputs(c), bmat)
        for b in range(nb):
            for ct in range(nct):
                bu_ref[ct, seq_rows(b, c), :] = bu[b * chunk:(b + 1) * chunk, ct * LANES:(ct + 1) * LANES]

    def scan(c, xr, xi):
        for k in range(c * chunk, (c + 1) * chunk):
            rows = pl.ds(k, nb, stride=pitch)
            for ct in range(half):
                r = ar[ct] * xr[ct] - ai[ct] * xi[ct] + bu_ref[ct, rows, :]
                i = ar[ct] * xi[ct] + ai[ct] * xr[ct] + bu_ref[half + ct, rows, :]
                bu_ref[ct, rows, :] = r
                bu_ref[half + ct, rows, :] = i
                xr[ct], xi[ct] = r, i
        return xr, xi

    def readout(c):
        x = jnp.concatenate([jnp.concatenate([bu_ref[ct, seq_rows(b, c), :] for ct in range(nct)], axis=1)
                             for b in range(nb)], axis=0)
        y = _dot(x.astype(BF16), cmat) + d_ref[...] * inputs(c).astype(F32)
        for b in range(nb):
            y_ref[b, c * chunk:(c + 1) * chunk, :] = y[b * chunk:(b + 1) * chunk].astype(y_ref.dtype)

    xr = [lanes(sr_ref[...], c) for c in range(half)]
    xi = [lanes(si_ref[...], c) for c in range(half)]
    nchunks = tt // chunk
    fill(0)
    for c in range(nchunks):
        if c + 1 < nchunks:
            fill(c + 1)
        xr, xi = scan(c, xr, xi)
        if c >= 1:
            readout(c - 1)
    readout(nchunks - 1)
    xr, xi = jnp.concatenate(xr, axis=1), jnp.concatenate(xi, axis=1)
    sr_ref[...] = xr
    si_ref[...] = xi

    @pl.when(t == pl.num_programs(1) - 1)
    def _():
        xr_ref[...] = xr
        xi_ref[...] = xi


def _ssm_prompt(z, ucol, bmat, ab_re, ab_im, cmat, d_skip, *, tt):
    nb, s, _ = z.shape
    nblk, ch, ns2 = bmat.shape
    ns = ns2 // 2
    d = nblk * ch
    assert s % tt == 0 and nb % 8 == 0
    pitch = tt + 8
    state_spec = pl.BlockSpec((nb, ns), lambda g, t: (0, g))
    state_shape = jax.ShapeDtypeStruct((nb, nblk * ns), F32)
    return pl.pallas_call(
        functools.partial(_ssm_prompt_kernel, tt=tt, nb=nb, pitch=pitch, chunk=_pick(tt, 32)),
        grid=(nblk, s // tt),
        in_specs=[pl.BlockSpec((nb, tt, ch), lambda g, t: (0, t, ucol * nblk + g)),
                  pl.BlockSpec((1, ch, ns2), lambda g, t: (g, 0, 0)),
                  pl.BlockSpec((1, 1, ns), lambda g, t: (g, 0, 0)),
                  pl.BlockSpec((1, 1, ns), lambda g, t: (g, 0, 0)),
                  pl.BlockSpec((1, ns2, ch), lambda g, t: (g, 0, 0)),
                  pl.BlockSpec((1, ch), lambda g, t: (0, g))],
        out_specs=[pl.BlockSpec((nb, tt, ch), lambda g, t: (0, t, g)), state_spec, state_spec],
        out_shape=[jax.ShapeDtypeStruct((nb, s, d), BF16), state_shape, state_shape],
        scratch_shapes=[pltpu.VMEM((ns2 // LANES, nb * pitch, LANES), F32),
                        pltpu.VMEM((nb, ns), F32), pltpu.VMEM((nb, ns), F32)],
        compiler_params=_params("parallel", "arbitrary"),
        name="ssm_prompt",
    )(z, bmat, ab_re, ab_im, cmat, d_skip)


def _post_kernel(x_ref, gate_ref, y_ref, o_ref, gatt_ref, gssm_ref, matt_ref, mssm_ref,
                 wglu_ref, bglu_ref, watt_ref, wssm_ref, wout_ref, lng_ref, lnb_ref, out_ref, *, alpha):
    g = jax.nn.gelu(y_ref[0].astype(F32))
    glu = g * jax.nn.sigmoid(_dot(g.astype(BF16), wglu_ref[...]) + bglu_ref[...])
    ys = glu * jax.nn.silu(gssm_ref[0].astype(F32))
    y_ssm = _dot(ys.astype(BF16), wssm_ref[...])
    ya = o_ref[0].astype(F32) * jax.nn.silu(gatt_ref[0].astype(F32))
    y_att = _dot(ya.astype(BF16), watt_ref[...])
    merged = (jax.nn.sigmoid(matt_ref[0].astype(F32)) * y_att
              + jax.nn.sigmoid(mssm_ref[0].astype(F32)) * y_ssm)
    r = alpha * x_ref[0] + gate_ref[0] * _dot(merged.astype(BF16), wout_ref[...])
    mu = jnp.mean(r, axis=-1, keepdims=True)
    cen = r - mu
    var = jnp.mean(cen * cen, axis=-1, keepdims=True)
    out_ref[0] = cen * lax.rsqrt(var + LN_EPS) * lng_ref[...] + lnb_ref[...]


def _post(x, gate, y, o_att, z, cols, w_glu, b_glu, w_att_out, w_ssm_out, w_out, ln_g, ln_b, *, tm, alpha):
    b, s, d = x.shape
    per_row = gate.shape[1] != 1
    row = lambda col: pl.BlockSpec((1, tm, d), lambda bi, m, col=col: (bi, m, col))
    gate_spec = row(0) if per_row else pl.BlockSpec((1, 1, d), lambda bi, m: (bi, 0, 0))
    mat = pl.BlockSpec((d, d), lambda bi, m: (0, 0))
    vec = pl.BlockSpec((1, d), lambda bi, m: (0, 0))
    return pl.pallas_call(
        functools.partial(_post_kernel, alpha=alpha),
        grid=(b, s // tm),
        in_specs=[row(0), gate_spec, row(0), row(0)] + [row(c) for c in cols]
                 + [mat, vec, mat, mat, mat, vec, vec],
        out_specs=row(0),
        out_shape=jax.ShapeDtypeStruct((b, s, d), F32),
        compiler_params=_params("parallel", "parallel"),
        name="post",
    )(x, gate, y, o_att, z, z, z, z,
      w_glu.astype(BF16), b_glu.reshape(1, d), w_att_out.astype(BF16), w_ssm_out.astype(BF16),
      w_out.astype(BF16), ln_g.reshape(1, d), ln_b.reshape(1, d))


def _pick(n, target):
    t = min(n, target)
    while n % t:
        t //= 2
    return t


def kernel(x_prompt, x_sample, c_prompt, c_sample, cache_k, cache_v, state_ssm_re, state_ssm_im, page_table, w_cond, b_cond, w_in, sb_bias, ssm_a_re, ssm_a_im, ssm_log_dt, ssm_b_re, ssm_b_im, ssm_c_re, ssm_c_im, ssm_d, w_glu, b_glu, w_att_out, w_ssm_out, w_out, ln_g, ln_b):
    depth = w_in.shape[0]
    assert depth == 1, "single-layer trunk"
    bsz, seq, d = x_prompt.shape
    dbsz, dseq, _ = x_sample.shape
    assert dseq == 1 and w_in.shape[2] == 8 * d
    n_heads = sb_bias.shape[-1]
    hd = cache_k.shape[-1]
    n_groups, n_state = ssm_a_re.shape[1:]
    alpha = (2.0 * depth) ** 0.25
    l = 0

    mod = _modulation(jnp.concatenate([c_prompt, c_sample], axis=0), w_cond[l], b_cond[l])
    shift, scale, gate = mod[:, :d], mod[:, d:2 * d], mod[:, 2 * d:]
    row3 = lambda a, lo, hi, shape: a[lo:hi].reshape(shape)
    p_shape, s_shape = (bsz, 1, d), (1, dbsz, d)

    ab_re, ab_im, bb_re, bb_im = _ssm_discretize(ssm_a_re[l], ssm_a_im[l], ssm_log_dt[l],
                                                 ssm_b_re[l], ssm_b_im[l])
    gpb = GROUPS_PER_BLOCK
    nblk = n_groups // gpb
    bmat = jnp.concatenate([_block_diag(bb_re, gpb), _block_diag(bb_im, gpb)], axis=-1)
    cmat = jnp.concatenate([_block_diag(ssm_c_re[l].transpose(0, 2, 1), gpb),
                            _block_diag(-ssm_c_im[l].transpose(0, 2, 1), gpb)], axis=1)
    a_blk = lambda a: a.reshape(nblk, 1, gpb * n_state)
    d_skip = ssm_d[l].reshape(1, d)

    w = w_in[l]
    wt = w[:, :3 * d].reshape(d, 3, d).transpose(1, 2, 0)
    wt_kvq = jnp.stack([wt[1], wt[2], wt[0]])

    kt_p, vt_p, z_p = _inproj_prompt(x_prompt, row3(scale, 0, bsz, p_shape), row3(shift, 0, bsz, p_shape),
                                     w.astype(BF16), wt_kvq[:2].astype(BF16),
                                     tm=_pick(seq, 512), cols=(0, 3, 4, 5, 6, 7))
    x_s = x_sample.reshape(dbsz, d)
    kvqt_s, z_s = _inproj_sample(x_s, scale[bsz:], shift[bsz:], w, wt_kvq)
    z_s = z_s.reshape(1, dbsz, 5 * d)

    pool = lambda c: c[l].transpose(0, 2, 3, 1)
    o_p, o_s = _attention(z_p, kt_p, vt_p, sb_bias[l], kvqt_s[2], pool(cache_k), pool(cache_v), page_table,
                          d=d, hd=hd, tq=_pick(seq, 256), nh=4)

    y_p, sr_p, si_p = _ssm_prompt(z_p, 2, bmat, a_blk(ab_re), a_blk(ab_im), cmat, d_skip, tt=_pick(seq, 256))
    out_p = _post(x_prompt, row3(gate, 0, bsz, p_shape), y_p, o_p, z_p, (1, 3, 4, 5),
                  w_glu[l], b_glu[l], w_att_out[l], w_ssm_out[l], w_out[l], ln_g[l], ln_b[l],
                  tm=_pick(seq, 512), alpha=alpha)
    y_s, sr_s, si_s = _ssm_scan(z_s[:, :, d:2 * d], state_ssm_re[l].reshape(dbsz, -1),
                                state_ssm_im[l].reshape(dbsz, -1), bmat, a_blk(ab_re), a_blk(ab_im),
                                cmat, d_skip, tt=1, precise=True)
    out_s = _post(x_s.reshape(1, dbsz, d), row3(gate, bsz, bsz + dbsz, s_shape), y_s, o_s.reshape(1, dbsz, d),
                  z_s, (0, 2, 3, 4),
                  w_glu[l], b_glu[l], w_att_out[l], w_ssm_out[l], w_out[l], ln_g[l], ln_b[l],
                  tm=dbsz, alpha=alpha)

    heads_t = lambda t, b, s: t.reshape(1, b, n_heads, hd, s).transpose(0, 1, 4, 2, 3)
    state = lambda t, b: t.reshape(1, b, n_groups, n_state)
    return (out_p, out_s.reshape(dbsz, 1, d),
            heads_t(kt_p, bsz, seq), heads_t(vt_p, bsz, seq), state(sr_p, bsz), state(si_p, bsz),
            heads_t(kvqt_s[0], 1, dbsz).reshape(1, dbsz, 1, n_heads, hd),
            heads_t(kvqt_s[1], 1, dbsz).reshape(1, dbsz, 1, n_heads, hd),
            state(sr_s, dbsz), state(si_s, dbsz))
```

```python
import functools
import math

import jax
import jax.numpy as jnp
from jax import lax
from jax.experimental import pallas as pl
from jax.experimental.pallas import tpu as pltpu

F32 = jnp.float32
BF16 = jnp.bfloat16
LN_EPS = 1e-5
LOG2E = math.log2(math.e)
LANES = 128
GROUPS_PER_BLOCK = 8
VMEM_LIMIT = 56 * 1024 * 1024
NT_DIMS = (((1,), (1,)), ((), ()))


def _dot(a, b, dims=None):
    if dims is None:
        return jnp.dot(a, b, preferred_element_type=F32)
    return lax.dot_general(a, b, dims, preferred_element_type=F32)


def _split(a):
    hi = a.astype(BF16)
    lo = (a - hi.astype(F32)).astype(BF16)
    return hi, lo


def _dot3(a, b, dims=None):
    ah, al = _split(a)
    bh, bl = _split(b)
    return _dot(ah, bh, dims) + (_dot(ah, bl, dims) + _dot(al, bh, dims))


def _dot_hilo(a, b_bf16):
    hi, lo = _split(a)
    return _dot(hi, b_bf16) + _dot(lo, b_bf16)


def _softplus2(y):
    return jnp.maximum(y, 0.0) + jnp.log2(1.0 + jnp.exp2(-jnp.abs(y)))


def _params(*sem):
    return pltpu.CompilerParams(dimension_semantics=sem, vmem_limit_bytes=VMEM_LIMIT)


def _mod_kernel(c_ref, w_ref, b_ref, o_ref):
    o_ref[...] = _dot3(c_ref[...], w_ref[...]) + b_ref[...]


def _modulation(c, w_cond, b_cond):
    rows, d = c.shape
    n = w_cond.shape[1]
    tn = 768 if n % 768 == 0 else n
    return pl.pallas_call(
        _mod_kernel,
        grid=(n // tn,),
        in_specs=[pl.BlockSpec((rows, d), lambda j: (0, 0)),
                  pl.BlockSpec((d, tn), lambda j: (0, j)),
                  pl.BlockSpec((1, tn), lambda j: (0, j))],
        out_specs=pl.BlockSpec((rows, tn), lambda j: (0, j)),
        out_shape=jax.ShapeDtypeStruct((rows, n), F32),
        compiler_params=_params("arbitrary"),
        name="mod",
    )(c, w_cond, b_cond.reshape(1, n))


def _inproj_prompt_kernel(x_ref, scale_ref, shift_ref, w_ref, wt_ref, kt_ref, vt_ref, z_ref, *, d, cols):
    h = (x_ref[0] * (1.0 + scale_ref[0]) + shift_ref[0]).astype(BF16)
    kt_ref[0] = _dot(wt_ref[0], h, NT_DIMS)
    vt_ref[0] = _dot(wt_ref[1], h, NT_DIMS)
    for gi, c in enumerate(cols):
        z_ref[0, :, gi * d:(gi + 1) * d] = _dot(h, w_ref[:, c * d:(c + 1) * d]).astype(z_ref.dtype)


def _inproj_prompt(x, scale, shift, w, wt, *, tm, cols):
    b, s, d = x.shape
    assert s % tm == 0
    resident = lambda shape: pl.BlockSpec(shape, lambda bi, m: (0,) * len(shape),
                                          pipeline_mode=pl.Buffered(1))
    mod_spec = pl.BlockSpec((1, 1, d), lambda bi, m: (bi, 0, 0))
    t_spec = pl.BlockSpec((1, d, tm), lambda bi, m: (bi, 0, m))
    return pl.pallas_call(
        functools.partial(_inproj_prompt_kernel, d=d, cols=cols),
        grid=(b, s // tm),
        in_specs=[pl.BlockSpec((1, tm, d), lambda bi, m: (bi, m, 0)), mod_spec, mod_spec,
                  resident(w.shape), resident(wt.shape)],
        out_specs=[t_spec, t_spec, pl.BlockSpec((1, tm, len(cols) * d), lambda bi, m: (bi, m, 0))],
        out_shape=[jax.ShapeDtypeStruct((b, d, s), F32), jax.ShapeDtypeStruct((b, d, s), F32),
                   jax.ShapeDtypeStruct((b, s, len(cols) * d), BF16)],
        compiler_params=_params("parallel", "parallel"),
        name="inproj_prompt",
    )(x, scale, shift, w, wt)


def _inproj_sample_kernel(x_ref, scale_ref, shift_ref, w_ref, wt_ref, t_ref, z_ref, h_ref, *, nt):
    j = pl.program_id(0)

    @pl.when(j == 0)
    def _():
        h_ref[...] = x_ref[...] * (1.0 + scale_ref[...]) + shift_ref[...]

    @pl.when(j < nt)
    def _():
        t_ref[0] = _dot3(wt_ref[0], h_ref[...], NT_DIMS)

    @pl.when(j >= nt)
    def _():
        z_ref[...] = _dot3(h_ref[...], w_ref[...])


def _inproj_sample(x, scale, shift, w, wt):
    r, d = x.shape
    ng, nt = w.shape[1] // d, wt.shape[0]
    rows = pl.BlockSpec((r, d), lambda j: (0, 0))
    return pl.pallas_call(
        functools.partial(_inproj_sample_kernel, nt=nt),
        grid=(ng,),
        in_specs=[rows, rows, rows,
                  pl.BlockSpec((d, d), lambda j: (0, jnp.maximum(j, nt))),
                  pl.BlockSpec((1, d, d), lambda j: (jnp.minimum(j, nt - 1), 0, 0))],
        out_specs=[pl.BlockSpec((1, d, r), lambda j: (jnp.minimum(j, nt - 1), 0, 0)),
                   pl.BlockSpec((r, d), lambda j: (0, jnp.maximum(j - nt, 0)))],
        out_shape=[jax.ShapeDtypeStruct((nt, d, r), F32),
                   jax.ShapeDtypeStruct((r, (ng - nt) * d), F32)],
        scratch_shapes=[pltpu.VMEM((r, d), F32)],
        compiler_params=_params("arbitrary"),
        name="inproj_sample",
    )(x, scale, shift, w, wt)


def _decode_begin(seq, qt_ref, qb_ref, acc_ref, run_ref, *, hd):
    nb, page = qt_ref.shape[1], acc_ref.shape[-1]
    onehot = (lax.broadcasted_iota(jnp.int32, (nb, page), 0) == seq).astype(BF16)
    qb = _dot_hilo(qt_ref[...], onehot) * (hd ** -0.5 * LOG2E)
    qb_ref[...] = qb.reshape(qb_ref.shape)
    acc_ref[...] = jnp.zeros_like(acc_ref)
    run_ref[...] = jnp.zeros_like(run_ref)


def _decode_step(bias_ref, tri_ref, k_refs, v_refs, qb_ref, acc_ref, run_ref):
    nh, _, page = acc_ref.shape
    npg = len(k_refs)
    qb = qb_ref[...]
    bias2 = bias_ref[...] * LOG2E
    tri = tri_ref[...]
    run = run_ref[...]
    weights = []
    for kr in k_refs:
        y = jnp.sum(kr[...] * qb, axis=1) + bias2
        sp = _softplus2(y)
        ext = _dot_hilo(sp, tri)
        weights.append(jnp.exp2(y - sp - ext[:, :page] - run))
        run = run + ext[:, page:]
    run_ref[...] = run
    for h in range(nh):
        part = v_refs[0][h] * weights[0][h:h + 1, :]
        for i in range(1, npg):
            part = part + v_refs[i][h] * weights[i][h:h + 1, :]
        acc_ref[h] += part


def _attn_kernel(pt_ref, bias_ref, q_ref, kt_ref, vt_ref, tri_ref, qt_ref, dbias_ref, dtri_ref, ck_ref, cv_ref,
                 o_ref, od_ref, y_ref, incl_ref, acc_ref, run_ref, ktb_ref, vtb_ref,
                 qb_ref, dacc_ref, drun_ref, kbuf_ref, vbuf_ref, sem_ref, *, tq, hd, nh, npg, dsteps):
    g = pl.program_id(1)
    i = pl.program_id(2)
    n = (pl.program_id(0) * pl.num_programs(1) + g) * pl.num_programs(2) + i
    steps = pl.num_programs(0) * pl.num_programs(1) * pl.num_programs(2)
    slot = n % 2
    assert 2 * hd == LANES and nh % 2 == 0

    def page_copies(step, slot):
        seq, first = step // dsteps, (step % dsteps) * npg
        n_pages = pt_ref.shape[1]
        copies = []
        for k in range(npg):
            page = pt_ref[seq, n_pages - 1 - (first + k)]
            copies.append(pltpu.make_async_copy(ck_ref.at[page], kbuf_ref.at[slot, k], sem_ref.at[slot, 0]))
            copies.append(pltpu.make_async_copy(cv_ref.at[page], vbuf_ref.at[slot, k], sem_ref.at[slot, 1]))
        return copies

    @pl.when(n == 0)
    def _():
        for c in page_copies(0, 0):
            c.start()

    @pl.when(n + 1 < steps)
    def _():
        for c in page_copies(n + 1, 1 - slot):
            c.start()

    for c in page_copies(n, slot):
        c.wait()
    k_pages = [kbuf_ref.at[slot, k] for k in range(npg)]
    v_pages = [vbuf_ref.at[slot, k] for k in range(npg)]

    @pl.when(n % dsteps == 0)
    def _():
        _decode_begin(n // dsteps, qt_ref, qb_ref, dacc_ref, drun_ref, hd=hd)

    @pl.when(i == 0)
    def _():
        s = kt_ref.shape[-1]
        r = lax.broadcasted_iota(jnp.int32, (hd, s), 0)
        for hh in range(nh):
            rows = slice(hh * hd, (hh + 1) * hd)
            keys = (kt_ref[0, rows, :] * (hd ** -0.5 * LOG2E)).astype(BF16)
            bias2 = jnp.full((hd, s), bias_ref[g * nh + hh] * LOG2E, F32)
            b_hi = bias2.astype(BF16).astype(F32)
            bias_rows = jnp.where(r == 0, b_hi, jnp.where(r == 1, bias2 - b_hi, 0.0)).astype(BF16)
            pair = [keys, bias_rows] if hh % 2 == 0 else [bias_rows, keys]
            ktb_ref[hh * LANES:(hh + 1) * LANES, :] = jnp.concatenate(pair, axis=0)
        vtb_ref[...] = vt_ref[0].astype(BF16)

    tri = tri_ref[...]
    row = lax.broadcasted_iota(jnp.int32, (tq, tq), 0)
    col = lax.broadcasted_iota(jnp.int32, (tq, tq), 1)
    below_diag = col < row
    lane = lax.broadcasted_iota(jnp.int32, (tq, LANES), 1)
    q_ext = []
    for hh in range(nh):
        tile = q_ref[0, :, (hh // 2) * LANES:(hh // 2 + 1) * LANES].astype(F32)
        q_ext.append(jnp.where(lane < hd if hh % 2 == 0 else lane >= hd, tile, 1.0).astype(BF16))

    def block_start(jj):
        return pl.multiple_of((i - jj) * tq, tq)

    def logits(jj, slot, diagonal=False):
        parts = []
        for hh in range(nh):
            y = _dot(q_ext[hh], ktb_ref[hh * LANES:(hh + 1) * LANES, pl.ds(block_start(jj), tq)])
            if diagonal:
                y = jnp.where(below_diag, y, -1e30)
            y_ref[slot, hh * tq:(hh + 1) * tq] = y
            parts.append(jnp.concatenate(_split(_softplus2(y)), axis=1))
        incl_ref[slot] = _dot(jnp.concatenate(parts, axis=0), tri)

    def weights(jj, slot):
        for hh in range(nh):
            vt = vtb_ref[hh * hd:(hh + 1) * hd, pl.ds(block_start(jj), tq)]
            incl = incl_ref[slot, hh * tq:(hh + 1) * tq]
            run = run_ref[hh]
            w = jnp.exp2(y_ref[slot, hh * tq:(hh + 1) * tq] - incl
                         - jnp.concatenate([run] * (tq // LANES), axis=1))
            acc_ref[hh] += _dot(w.astype(BF16), vt, NT_DIMS)
            run_ref[hh] = run + incl[:, :1]

    _decode_step(dbias_ref, dtri_ref, k_pages, v_pages, qb_ref, dacc_ref, drun_ref)
    acc_ref[...] = jnp.zeros_like(acc_ref)
    run_ref[...] = jnp.zeros_like(run_ref)
    logits(0, 0, diagonal=True)

    @pl.loop(0, i // 2)
    def _(m):
        logits(2 * m + 1, 1)
        weights(2 * m, 0)
        logits(2 * m + 2, 0)
        weights(2 * m + 1, 1)

    @pl.when(i % 2 == 1)
    def _():
        logits(i, 1)
        weights(i - 1, 0)
        weights(i, 1)

    @pl.when(i % 2 == 0)
    def _():
        weights(i, 0)

    o_ref[0] = jnp.concatenate([acc_ref[hh] for hh in range(nh)], axis=-1).astype(o_ref.dtype)

    @pl.when(n % dsteps == dsteps - 1)
    def _():
        od_ref[0] = jnp.sum(dacc_ref[...], axis=-1)


def _attention(z, kt, vt, sb_bias, qt, cache_kt, cache_vt, page_table, *, d, hd, tq, nh):
    b, s, _ = z.shape
    width = nh * hd
    grid = (b, d // width, s // tq)
    r = lax.broadcasted_iota(jnp.int32, (2 * tq, tq), 0) % tq
    c = lax.broadcasted_iota(jnp.int32, (2 * tq, tq), 1)
    tri = (r >= c).astype(BF16)

    _, bs = qt.shape
    _, n_heads, _, page = cache_kt.shape
    n_pages = page_table.shape[1]
    steps = grid[0] * grid[1] * grid[2]
    assert steps % bs == 0 and n_pages % (steps // bs) == 0 and page == LANES
    dsteps = steps // bs
    npg = n_pages // dsteps
    r = lax.broadcasted_iota(jnp.int32, (page, 2 * page), 0)
    c = lax.broadcasted_iota(jnp.int32, (page, 2 * page), 1)
    dtri = jnp.logical_or(r > c, c >= page).astype(BF16)
    dbias = jnp.broadcast_to(sb_bias.astype(F32)[:, None], (n_heads, page))

    def step_of(bi, g, i):
        return (bi * grid[1] + g) * grid[2] + i

    const = lambda bi, g, i, pt: (0, 0)
    page_buffers = pltpu.VMEM((2, npg, n_heads, hd, page), F32)
    grid_spec = pltpu.PrefetchScalarGridSpec(
        num_scalar_prefetch=1,
        grid=grid,
        in_specs=[pl.BlockSpec(memory_space=pltpu.SMEM),
                  pl.BlockSpec((1, tq, width), lambda bi, g, i, pt: (bi, i, g)),
                  pl.BlockSpec((1, width, s), lambda bi, g, i, pt: (bi, g, 0)),
                  pl.BlockSpec((1, width, s), lambda bi, g, i, pt: (bi, g, 0)),
                  pl.BlockSpec((2 * tq, tq), const),
                  pl.BlockSpec((d, bs), const), pl.BlockSpec((n_heads, page), const),
                  pl.BlockSpec((page, 2 * page), const),
                  pl.BlockSpec(memory_space=pl.ANY), pl.BlockSpec(memory_space=pl.ANY)],
        out_specs=[pl.BlockSpec((1, tq, width), lambda bi, g, i, pt: (bi, i, g)),
                   pl.BlockSpec((1, n_heads, hd), lambda bi, g, i, pt: (step_of(bi, g, i) // dsteps, 0, 0))],
        scratch_shapes=[pltpu.VMEM((2, nh * tq, tq), F32), pltpu.VMEM((2, nh * tq, tq), F32),
                        pltpu.VMEM((nh, tq, hd), F32), pltpu.VMEM((nh, tq, LANES), F32),
                        pltpu.VMEM((nh * LANES, s), BF16), pltpu.VMEM((width, s), BF16),
                        pltpu.VMEM((n_heads, hd, page), F32), pltpu.VMEM((n_heads, hd, page), F32),
                        pltpu.VMEM((n_heads, page), F32),
                        page_buffers, page_buffers, pltpu.SemaphoreType.DMA((2, 2))],
    )
    return pl.pallas_call(
        functools.partial(_attn_kernel, tq=tq, hd=hd, nh=nh, npg=npg, dsteps=dsteps),
        grid_spec=grid_spec,
        out_shape=[jax.ShapeDtypeStruct((b, s, d), BF16), jax.ShapeDtypeStruct((bs, n_heads, hd), F32)],
        compiler_params=_params("arbitrary", "arbitrary", "arbitrary"),
        name="attn",
    )(page_table, sb_bias.astype(F32), z, kt, vt, tri, qt, dbias, dtri, cache_kt, cache_vt)


def _ssmprep_kernel(ar_ref, ai_ref, ldt_ref, br_ref, bi_ref, abr_ref, abi_ref, bbr_ref, bbi_ref):
    dt = jnp.exp(ldt_ref[...])
    lr, li = ar_ref[...], ai_ref[...]
    mag = jnp.exp(lr * dt)
    ab_re, ab_im = mag * jnp.cos(li * dt), mag * jnp.sin(li * dt)
    den = lr * lr + li * li
    nr = ab_re - 1.0
    co_re = (nr * lr + ab_im * li) / den
    co_im = (ab_im * lr - nr * li) / den
    abr_ref[...] = ab_re
    abi_ref[...] = ab_im
    br, bi = br_ref[...], bi_ref[...]
    bbr_ref[...] = co_re * br - co_im * bi
    bbi_ref[...] = co_re * bi + co_im * br


def _ssm_discretize(a_re, a_im, log_dt, b_re, b_im):
    g, p = a_re.shape
    c = b_re.shape[-1]
    rep = lambda a: jnp.repeat(a, c, axis=0)
    rows = lambda b: b.transpose(0, 2, 1).reshape(g * c, p)
    out = jax.ShapeDtypeStruct((g * c, p), F32)
    ab_re, ab_im, bb_re, bb_im = pl.pallas_call(
        _ssmprep_kernel, out_shape=[out] * 4, name="ssmprep",
    )(rep(a_re), rep(a_im), rep(log_dt.reshape(g, 1)), rows(b_re), rows(b_im))
    return ab_re[::c], ab_im[::c], bb_re.reshape(g, c, p), bb_im.reshape(g, c, p)


def _block_diag(w, gpb):
    g, r, k = w.shape
    w = w.reshape(g // gpb, gpb, r, k)
    eye = jnp.eye(gpb, dtype=w.dtype)
    return jnp.einsum("bgrk,gh->bgrhk", w, eye).reshape(g // gpb, gpb * r, gpb * k)


def _ssm_kernel(u_ref, h0r_ref, h0i_ref, bmat_ref, ar_ref, ai_ref, cmat_ref, d_ref,
                y_ref, xr_ref, xi_ref, bu_ref, sr_ref, si_ref, *, tt, nb, ns, precise):
    t = pl.program_id(1)

    @pl.when(t == 0)
    def _():
        sr_ref[...] = h0r_ref[...]
        si_ref[...] = h0i_ref[...]

    u = u_ref[...].reshape(tt * nb, u_ref.shape[-1])
    if precise:
        bu_ref[...] = _dot3(u, bmat_ref[0])
    else:
        bu_ref[...] = _dot(u.astype(BF16), bmat_ref[0].astype(BF16))
    ar = jnp.broadcast_to(ar_ref[0], (nb, ns))
    ai = jnp.broadcast_to(ai_ref[0], (nb, ns))

    def step(k, carry):
        xr, xi = carry
        rows = pl.ds(pl.multiple_of(k * nb, nb), nb)
        nr = ar * xr - ai * xi + bu_ref[rows, :ns]
        ni = ar * xi + ai * xr + bu_ref[rows, ns:]
        bu_ref[rows, :ns] = nr
        bu_ref[rows, ns:] = ni
        return nr, ni

    xr, xi = lax.fori_loop(0, tt, step, (sr_ref[...], si_ref[...]))
    sr_ref[...] = xr
    si_ref[...] = xi
    if precise:
        y = _dot3(bu_ref[...], cmat_ref[0])
    else:
        y = _dot(bu_ref[...].astype(BF16), cmat_ref[0].astype(BF16))
    y = y + d_ref[...] * u
    y_ref[...] = y.reshape(y_ref.shape).astype(y_ref.dtype)

    @pl.when(t == pl.num_programs(1) - 1)
    def _():
        xr_ref[...] = xr
        xi_ref[...] = xi


def _ssm_scan(u, h0_re, h0_im, bmat, ab_re, ab_im, cmat, d_skip, *, tt, precise):
    s, nb, d = u.shape
    nblk, ch, ns2 = bmat.shape
    ns = ns2 // 2
    assert s % tt == 0 and d == nblk * ch
    grid = (nblk, s // tt)
    state_spec = pl.BlockSpec((nb, ns), lambda g, t: (0, g))
    return pl.pallas_call(
        functools.partial(_ssm_kernel, tt=tt, nb=nb, ns=ns, precise=precise),
        grid=grid,
        in_specs=[pl.BlockSpec((tt, nb, ch), lambda g, t: (t, 0, g)),
                  state_spec, state_spec,
                  pl.BlockSpec((1, ch, ns2), lambda g, t: (g, 0, 0)),
                  pl.BlockSpec((1, 1, ns), lambda g, t: (g, 0, 0)),
                  pl.BlockSpec((1, 1, ns), lambda g, t: (g, 0, 0)),
                  pl.BlockSpec((1, ns2, ch), lambda g, t: (g, 0, 0)),
                  pl.BlockSpec((1, ch), lambda g, t: (0, g))],
        out_specs=[pl.BlockSpec((tt, nb, ch), lambda g, t: (t, 0, g)), state_spec, state_spec],
        out_shape=[jax.ShapeDtypeStruct((s, nb, d), F32),
                   jax.ShapeDtypeStruct(h0_re.shape, F32),
                   jax.ShapeDtypeStruct(h0_re.shape, F32)],
        scratch_shapes=[pltpu.VMEM((tt * nb, ns2), F32),
                        pltpu.VMEM((nb, ns), F32), pltpu.VMEM((nb, ns), F32)],
        compiler_params=_params("parallel", "arbitrary"),
        name="ssm",
    )(u, h0_re, h0_im, bmat, ab_re, ab_im, cmat, d_skip)


def _ssm_prompt_kernel(u_ref, bmat_ref, ar_ref, ai_ref, cmat_ref, d_ref, y_ref, xr_ref, xi_ref,
                       bu_ref, sr_ref, si_ref, *, tt, nb, pitch, chunk):
    t = pl.program_id(1)

    @pl.when(t == 0)
    def _():
        sr_ref[...] = jnp.zeros_like(sr_ref)
        si_ref[...] = jnp.zeros_like(si_ref)

    nct = bu_ref.shape[0]
    half = nct // 2
    bmat = bmat_ref[0].astype(BF16)
    cmat = cmat_ref[0].astype(BF16)
    lanes = lambda a, c: a[:, c * LANES:(c + 1) * LANES]
    ar = [jnp.broadcast_to(lanes(ar_ref[0], c), (nb, LANES)) for c in range(half)]
    ai = [jnp.broadcast_to(lanes(ai_ref[0], c), (nb, LANES)) for c in range(half)]
    seq_rows = lambda b, c: slice(b * pitch + c * chunk, b * pitch + (c + 1) * chunk)

    def inputs(c):
        return jnp.concatenate([u_ref[b, c * chunk:(c + 1) * chunk, :] for b in range(nb)], axis=0)

    def fill(c):
        bu = _dot(inputs(c), bmat)
        for b in range(nb):
            for ct in range(nct):
                bu_ref[ct, seq_rows(b, c), :] = bu[b * chunk:(b + 1) * chunk, ct * LANES:(ct + 1) * LANES]

    def scan(c, xr, xi):
        for k in range(c * chunk, (c + 1) * chunk):
            rows = pl.ds(k, nb, stride=pitch)
            for ct in range(half):
                r = ar[ct] * xr[ct] - ai[ct] * xi[ct] + bu_ref[ct, rows, :]
                i = ar[ct] * xi[ct] + ai[ct] * xr[ct] + bu_ref[half + ct, rows, :]
                bu_ref[ct, rows, :] = r
                bu_ref[half + ct, rows, :] = i
                xr[ct], xi[ct] = r, i
        return xr, xi

    def readout(c):
        x = jnp.concatenate([jnp.concatenate([bu_ref[ct, seq_rows(b, c), :] for ct in range(nct)], axis=1)
                             for b in range(nb)], axis=0)
        y = _dot(x.astype(BF16), cmat) + d_ref[...] * inputs(c).astype(F32)
        for b in range(nb):
            y_ref[b, c * chunk:(c + 1) * chunk, :] = y[b * chunk:(b + 1) * chunk].astype(y_ref.dtype)

    xr = [lanes(sr_ref[...], c) for c in range(half)]
    xi = [lanes(si_ref[...], c) for c in range(half)]
    nchunks = tt // chunk
    fill(0)
    for c in range(nchunks):
        if c + 1 < nchunks:
            fill(c + 1)
        xr, xi = scan(c, xr, xi)
        if c >= 1:
            readout(c - 1)
    readout(nchunks - 1)
    xr, xi = jnp.concatenate(xr, axis=1), jnp.concatenate(xi, axis=1)
    sr_ref[...] = xr
    si_ref[...] = xi

    @pl.when(t == pl.num_programs(1) - 1)
    def _():
        xr_ref[...] = xr
        xi_ref[...] = xi


def _ssm_prompt(z, ucol, bmat, ab_re, ab_im, cmat, d_skip, *, tt):
    nb, s, _ = z.shape
    nblk, ch, ns2 = bmat.shape
    ns = ns2 // 2
    d = nblk * ch
    assert s % tt == 0 and nb % 8 == 0
    pitch = tt + 8
    state_spec = pl.BlockSpec((nb, ns), lambda g, t: (0, g))
    state_shape = jax.ShapeDtypeStruct((nb, nblk * ns), F32)
    return pl.pallas_call(
        functools.partial(_ssm_prompt_kernel, tt=tt, nb=nb, pitch=pitch, chunk=_pick(tt, 64)),
        grid=(nblk, s // tt),
        in_specs=[pl.BlockSpec((nb, tt, ch), lambda g, t: (0, t, ucol * nblk + g)),
                  pl.BlockSpec((1, ch, ns2), lambda g, t: (g, 0, 0)),
                  pl.BlockSpec((1, 1, ns), lambda g, t: (g, 0, 0)),
                  pl.BlockSpec((1, 1, ns), lambda g, t: (g, 0, 0)),
                  pl.BlockSpec((1, ns2, ch), lambda g, t: (g, 0, 0)),
                  pl.BlockSpec((1, ch), lambda g, t: (0, g))],
        out_specs=[pl.BlockSpec((nb, tt, ch), lambda g, t: (0, t, g)), state_spec, state_spec],
        out_shape=[jax.ShapeDtypeStruct((nb, s, d), BF16), state_shape, state_shape],
        scratch_shapes=[pltpu.VMEM((ns2 // LANES, nb * pitch, LANES), F32),
                        pltpu.VMEM((nb, ns), F32), pltpu.VMEM((nb, ns), F32)],
        compiler_params=_params("parallel", "arbitrary"),
        name="ssm_prompt",
    )(z, bmat, ab_re, ab_im, cmat, d_skip)


def _post_kernel(x_ref, gate_ref, y_ref, o_ref, gatt_ref, gssm_ref, matt_ref, mssm_ref,
                 wglu_ref, bglu_ref, watt_ref, wssm_ref, wout_ref, lng_ref, lnb_ref, out_ref, *, alpha):
    g = jax.nn.gelu(y_ref[0].astype(F32))
    glu = g * jax.nn.sigmoid(_dot(g.astype(BF16), wglu_ref[...]) + bglu_ref[...])
    ys = glu * jax.nn.silu(gssm_ref[0].astype(F32))
    y_ssm = _dot(ys.astype(BF16), wssm_ref[...])
    ya = o_ref[0].astype(F32) * jax.nn.silu(gatt_ref[0].astype(F32))
    y_att = _dot(ya.astype(BF16), watt_ref[...])
    merged = (jax.nn.sigmoid(matt_ref[0].astype(F32)) * y_att
              + jax.nn.sigmoid(mssm_ref[0].astype(F32)) * y_ssm)
    r = alpha * x_ref[0] + gate_ref[0] * _dot(merged.astype(BF16), wout_ref[...])
    mu = jnp.mean(r, axis=-1, keepdims=True)
    cen = r - mu
    var = jnp.mean(cen * cen, axis=-1, keepdims=True)
    out_ref[0] = cen * lax.rsqrt(var + LN_EPS) * lng_ref[...] + lnb_ref[...]


def _post(x, gate, y, o_att, z, cols, w_glu, b_glu, w_att_out, w_ssm_out, w_out, ln_g, ln_b, *, tm, alpha):
    b, s, d = x.shape
    per_row = gate.shape[1] != 1
    row = lambda col: pl.BlockSpec((1, tm, d), lambda bi, m, col=col: (bi, m, col))
    gate_spec = row(0) if per_row else pl.BlockSpec((1, 1, d), lambda bi, m: (bi, 0, 0))
    mat = pl.BlockSpec((d, d), lambda bi, m: (0, 0))
    vec = pl.BlockSpec((1, d), lambda bi, m: (0, 0))
    return pl.pallas_call(
        functools.partial(_post_kernel, alpha=alpha),
        grid=(b, s // tm),
        in_specs=[row(0), gate_spec, row(0), row(0)] + [row(c) for c in cols]
                 + [mat, vec, mat, mat, mat, vec, vec],
        out_specs=row(0),
        out_shape=jax.ShapeDtypeStruct((b, s, d), F32),
        compiler_params=_params("parallel", "parallel"),
        name="post",
    )(x, gate, y, o_att, z, z, z, z,
      w_glu.astype(BF16), b_glu.reshape(1, d), w_att_out.astype(BF16), w_ssm_out.astype(BF16),
      w_out.astype(BF16), ln_g.reshape(1, d), ln_b.reshape(1, d))


def _pick(n, target):
    t = min(n, target)
    while n % t:
        t //= 2
    return t


def kernel(x_prompt, x_sample, c_prompt, c_sample, cache_k, cache_v, state_ssm_re, state_ssm_im, page_table, w_cond, b_cond, w_in, sb_bias, ssm_a_re, ssm_a_im, ssm_log_dt, ssm_b_re, ssm_b_im, ssm_c_re, ssm_c_im, ssm_d, w_glu, b_glu, w_att_out, w_ssm_out, w_out, ln_g, ln_b):
    depth = w_in.shape[0]
    assert depth == 1, "single-layer trunk"
    bsz, seq, d = x_prompt.shape
    dbsz, dseq, _ = x_sample.shape
    assert dseq == 1 and w_in.shape[2] == 8 * d
    n_heads = sb_bias.shape[-1]
    hd = cache_k.shape[-1]
    n_groups, n_state = ssm_a_re.shape[1:]
    alpha = (2.0 * depth) ** 0.25
    l = 0

    mod = _modulation(jnp.concatenate([c_prompt, c_sample], axis=0), w_cond[l], b_cond[l])
    shift, scale, gate = mod[:, :d], mod[:, d:2 * d], mod[:, 2 * d:]
    row3 = lambda a, lo, hi, shape: a[lo:hi].reshape(shape)
    p_shape, s_shape = (bsz, 1, d), (1, dbsz, d)

    ab_re, ab_im, bb_re, bb_im = _ssm_discretize(ssm_a_re[l], ssm_a_im[l], ssm_log_dt[l],
                                                 ssm_b_re[l], ssm_b_im[l])
    gpb = GROUPS_PER_BLOCK
    nblk = n_groups // gpb
    bmat = jnp.concatenate([_block_diag(bb_re, gpb), _block_diag(bb_im, gpb)], axis=-1)
    cmat = jnp.concatenate([_block_diag(ssm_c_re[l].transpose(0, 2, 1), gpb),
                            _block_diag(-ssm_c_im[l].transpose(0, 2, 1), gpb)], axis=1)
    a_blk = lambda a: a.reshape(nblk, 1, gpb * n_state)
    d_skip = ssm_d[l].reshape(1, d)

    w = w_in[l]
    wt_kvq = jnp.stack([w[:, d:2 * d].T, w[:, 2 * d:3 * d].T, w[:, :d].T])

    kt_p, vt_p, z_p = _inproj_prompt(x_prompt, row3(scale, 0, bsz, p_shape), row3(shift, 0, bsz, p_shape),
                                     w.astype(BF16), wt_kvq[:2].astype(BF16),
                                     tm=_pick(seq, 512), cols=(0, 3, 4, 5, 6, 7))
    x_s = x_sample.reshape(dbsz, d)
    kvqt_s, z_s = _inproj_sample(x_s, scale[bsz:], shift[bsz:], w, wt_kvq)
    z_s = z_s.reshape(1, dbsz, 5 * d)

    pool = lambda c: c[l].transpose(0, 2, 3, 1)
    o_p, o_s = _attention(z_p, kt_p, vt_p, sb_bias[l], kvqt_s[2], pool(cache_k), pool(cache_v), page_table,
                          d=d, hd=hd, tq=_pick(seq, 256), nh=4)

    y_p, sr_p, si_p = _ssm_prompt(z_p, 2, bmat, a_blk(ab_re), a_blk(ab_im), cmat, d_skip, tt=_pick(seq, 256))
    out_p = _post(x_prompt, row3(gate, 0, bsz, p_shape), y_p, o_p, z_p, (1, 3, 4, 5),
                  w_glu[l], b_glu[l], w_att_out[l], w_ssm_out[l], w_out[l], ln_g[l], ln_b[l],
                  tm=_pick(seq, 512), alpha=alpha)
    y_s, sr_s, si_s = _ssm_scan(z_s[:, :, d:2 * d], state_ssm_re[l].reshape(dbsz, -1),
                                state_ssm_im[l].reshape(dbsz, -1), bmat, a_blk(ab_re), a_blk(ab_im),
                                cmat, d_skip, tt=1, precise=True)
    out_s = _post(x_s.reshape(1, dbsz, d), row3(gate, bsz, bsz + dbsz, s_shape), y_s, o_s.reshape(1, dbsz, d),
                  z_s, (0, 2, 3, 4),
                  w_glu[l], b_glu[l], w_att_out[l], w_ssm_out[l], w_out[l], ln_g[l], ln_b[l],
                  tm=dbsz, alpha=alpha)

    heads_t = lambda t, b, s: t.reshape(1, b, n_heads, hd, s).transpose(0, 1, 4, 2, 3)
    state = lambda t, b: t.reshape(1, b, n_groups, n_state)
    return (out_p, out_s.reshape(dbsz, 1, d),
            heads_t(kt_p, bsz, seq), heads_t(vt_p, bsz, seq), state(sr_p, bsz), state(si_p, bsz),
            heads_t(kvqt_s[0], 1, dbsz).reshape(1, dbsz, 1, n_heads, hd),
            heads_t(kvqt_s[1], 1, dbsz).reshape(1, dbsz, 1, n_heads, hd),
            state(sr_s, dbsz), state(si_s, dbsz))
```

```python
import functools
import math

import jax
import jax.numpy as jnp
from jax import lax
from jax.experimental import pallas as pl
from jax.experimental.pallas import tpu as pltpu

F32 = jnp.float32
BF16 = jnp.bfloat16
LN_EPS = 1e-5
LOG2E = math.log2(math.e)
LANES = 128
GROUPS_PER_BLOCK = 8
VMEM_LIMIT = 56 * 1024 * 1024
NT_DIMS = (((1,), (1,)), ((), ()))


def _dot(a, b, dims=None):
    if dims is None:
        return jnp.dot(a, b, preferred_element_type=F32)
    return lax.dot_general(a, b, dims, preferred_element_type=F32)


def _split(a):
    hi = a.astype(BF16)
    lo = (a - hi.astype(F32)).astype(BF16)
    return hi, lo


def _dot3(a, b, dims=None):
    ah, al = _split(a)
    bh, bl = _split(b)
    return _dot(ah, bh, dims) + (_dot(ah, bl, dims) + _dot(al, bh, dims))


def _dot_hilo(a, b_bf16):
    hi, lo = _split(a)
    return _dot(hi, b_bf16) + _dot(lo, b_bf16)


def _softplus2(y):
    return jnp.maximum(y, 0.0) + jnp.log2(1.0 + jnp.exp2(-jnp.abs(y)))


def _params(*sem):
    return pltpu.CompilerParams(dimension_semantics=sem, vmem_limit_bytes=VMEM_LIMIT)


def _mod_kernel(c_ref, w_ref, b_ref, o_ref):
    o_ref[...] = _dot3(c_ref[...], w_ref[...]) + b_ref[...]


def _modulation(c, w_cond, b_cond):
    rows, d = c.shape
    n = w_cond.shape[1]
    tn = 768 if n % 768 == 0 else n
    return pl.pallas_call(
        _mod_kernel,
        grid=(n // tn,),
        in_specs=[pl.BlockSpec((rows, d), lambda j: (0, 0)),
                  pl.BlockSpec((d, tn), lambda j: (0, j)),
                  pl.BlockSpec((1, tn), lambda j: (0, j))],
        out_specs=pl.BlockSpec((rows, tn), lambda j: (0, j)),
        out_shape=jax.ShapeDtypeStruct((rows, n), F32),
        compiler_params=_params("arbitrary"),
        name="mod",
    )(c, w_cond, b_cond.reshape(1, n))


def _inproj_prompt_kernel(x_ref, scale_ref, shift_ref, w_ref, wt_ref, kt_ref, vt_ref, z_ref, *, d, cols):
    h = (x_ref[0] * (1.0 + scale_ref[0]) + shift_ref[0]).astype(BF16)
    kt_ref[0] = _dot(wt_ref[0], h, NT_DIMS)
    vt_ref[0] = _dot(wt_ref[1], h, NT_DIMS)
    for gi, c in enumerate(cols):
        z_ref[0, :, gi * d:(gi + 1) * d] = _dot(h, w_ref[:, c * d:(c + 1) * d]).astype(z_ref.dtype)


def _inproj_prompt(x, scale, shift, w, wt, *, tm, cols):
    b, s, d = x.shape
    assert s % tm == 0
    resident = lambda shape: pl.BlockSpec(shape, lambda bi, m: (0,) * len(shape),
                                          pipeline_mode=pl.Buffered(1))
    mod_spec = pl.BlockSpec((1, 1, d), lambda bi, m: (bi, 0, 0))
    t_spec = pl.BlockSpec((1, d, tm), lambda bi, m: (bi, 0, m))
    return pl.pallas_call(
        functools.partial(_inproj_prompt_kernel, d=d, cols=cols),
        grid=(b, s // tm),
        in_specs=[pl.BlockSpec((1, tm, d), lambda bi, m: (bi, m, 0)), mod_spec, mod_spec,
                  resident(w.shape), resident(wt.shape)],
        out_specs=[t_spec, t_spec, pl.BlockSpec((1, tm, len(cols) * d), lambda bi, m: (bi, m, 0))],
        out_shape=[jax.ShapeDtypeStruct((b, d, s), F32), jax.ShapeDtypeStruct((b, d, s), F32),
                   jax.ShapeDtypeStruct((b, s, len(cols) * d), BF16)],
        compiler_params=_params("parallel", "parallel"),
        name="inproj_prompt",
    )(x, scale, shift, w, wt)


def _inproj_sample_kernel(x_ref, scale_ref, shift_ref, w_ref, wt_ref, t_ref, z_ref, h_ref, *, nt):
    j = pl.program_id(0)

    @pl.when(j == 0)
    def _():
        h_ref[...] = x_ref[...] * (1.0 + scale_ref[...]) + shift_ref[...]

    @pl.when(j < nt)
    def _():
        t_ref[0] = _dot3(wt_ref[0], h_ref[...], NT_DIMS)

    @pl.when(j >= nt)
    def _():
        z_ref[...] = _dot3(h_ref[...], w_ref[...])


def _inproj_sample(x, scale, shift, w, wt):
    r, d = x.shape
    ng, nt = w.shape[1] // d, wt.shape[0]
    rows = pl.BlockSpec((r, d), lambda j: (0, 0))
    return pl.pallas_call(
        functools.partial(_inproj_sample_kernel, nt=nt),
        grid=(ng,),
        in_specs=[rows, rows, rows,
                  pl.BlockSpec((d, d), lambda j: (0, jnp.maximum(j, nt))),
                  pl.BlockSpec((1, d, d), lambda j: (jnp.minimum(j, nt - 1), 0, 0))],
        out_specs=[pl.BlockSpec((1, d, r), lambda j: (jnp.minimum(j, nt - 1), 0, 0)),
                   pl.BlockSpec((r, d), lambda j: (0, jnp.maximum(j - nt, 0)))],
        out_shape=[jax.ShapeDtypeStruct((nt, d, r), F32),
                   jax.ShapeDtypeStruct((r, (ng - nt) * d), F32)],
        scratch_shapes=[pltpu.VMEM((r, d), F32)],
        compiler_params=_params("arbitrary"),
        name="inproj_sample",
    )(x, scale, shift, w, wt)


def _decode_begin(seq, qt_ref, qb_ref, acc_ref, run_ref, *, hd):
    nb, page = qt_ref.shape[1], acc_ref.shape[-1]
    onehot = (lax.broadcasted_iota(jnp.int32, (nb, page), 0) == seq).astype(BF16)
    qb = _dot_hilo(qt_ref[...], onehot) * (hd ** -0.5 * LOG2E)
    qb_ref[...] = qb.reshape(qb_ref.shape)
    acc_ref[...] = jnp.zeros_like(acc_ref)
    run_ref[...] = jnp.zeros_like(run_ref)


def _decode_keys(bias_ref, tri_ref, k_refs, qb_ref, run_ref, w_ref):
    page = run_ref.shape[-1]
    qb = qb_ref[...]
    bias2 = bias_ref[...] * LOG2E
    tri = tri_ref[...]
    run = run_ref[...]
    for k, kr in enumerate(k_refs):
        y = jnp.sum(kr[...] * qb, axis=1) + bias2
        sp = _softplus2(y)
        ext = _dot_hilo(sp, tri)
        w_ref[k] = jnp.exp2(y - sp - ext[:, :page] - run)
        run = run + ext[:, page:]
    run_ref[...] = run


def _decode_values(v_refs, w_ref, acc_ref):
    weights = [w_ref[k] for k in range(len(v_refs))]
    for h in range(acc_ref.shape[0]):
        part = v_refs[0][h] * weights[0][h:h + 1, :]
        for k in range(1, len(v_refs)):
            part = part + v_refs[k][h] * weights[k][h:h + 1, :]
        acc_ref[h] += part


def _attn_kernel(pt_ref, bias_ref, q_ref, kt_ref, vt_ref, tri_ref, qt_ref, dbias_ref, dtri_ref, ck_ref, cv_ref,
                 o_ref, od_ref, y_ref, incl_ref, acc_ref, run_ref, ktb_ref, vtb_ref,
                 qb_ref, dacc_ref, drun_ref, dw_ref, kbuf_ref, vbuf_ref, sem_ref, *, tq, hd, nh, npg, dsteps):
    g = pl.program_id(1)
    i = pl.program_id(2)
    n = (pl.program_id(0) * pl.num_programs(1) + g) * pl.num_programs(2) + i
    steps = pl.num_programs(0) * pl.num_programs(1) * pl.num_programs(2)
    slot = n % 2
    assert 2 * hd == LANES and nh % 2 == 0

    def page_copies(step, slot):
        seq, first = step // dsteps, (step % dsteps) * npg
        n_pages = pt_ref.shape[1]
        copies = []
        for k in range(npg):
            page = pt_ref[seq, n_pages - 1 - (first + k)]
            copies.append(pltpu.make_async_copy(ck_ref.at[page], kbuf_ref.at[slot, k], sem_ref.at[slot, 0]))
            copies.append(pltpu.make_async_copy(cv_ref.at[page], vbuf_ref.at[slot, k], sem_ref.at[slot, 1]))
        return copies

    @pl.when(n == 0)
    def _():
        for c in page_copies(0, 0):
            c.start()

    @pl.when(n + 1 < steps)
    def _():
        for c in page_copies(n + 1, 1 - slot):
            c.start()

    for c in page_copies(n, slot):
        c.wait()
    k_pages = [kbuf_ref.at[slot, k] for k in range(npg)]
    v_pages = [vbuf_ref.at[slot, k] for k in range(npg)]

    @pl.when(n % dsteps == 0)
    def _():
        _decode_begin(n // dsteps, qt_ref, qb_ref, dacc_ref, drun_ref, hd=hd)

    @pl.when(i == 0)
    def _():
        s = kt_ref.shape[-1]
        r = lax.broadcasted_iota(jnp.int32, (hd, s), 0)
        for hh in range(nh):
            rows = slice(hh * hd, (hh + 1) * hd)
            keys = (kt_ref[0, rows, :] * (hd ** -0.5 * LOG2E)).astype(BF16)
            bias2 = jnp.full((hd, s), bias_ref[g * nh + hh] * LOG2E, F32)
            b_hi = bias2.astype(BF16).astype(F32)
            bias_rows = jnp.where(r == 0, b_hi, jnp.where(r == 1, bias2 - b_hi, 0.0)).astype(BF16)
            pair = [keys, bias_rows] if hh % 2 == 0 else [bias_rows, keys]
            ktb_ref[hh * LANES:(hh + 1) * LANES, :] = jnp.concatenate(pair, axis=0)
        vtb_ref[...] = vt_ref[0].astype(BF16)

    tri = tri_ref[...]
    row = lax.broadcasted_iota(jnp.int32, (tq, tq), 0)
    col = lax.broadcasted_iota(jnp.int32, (tq, tq), 1)
    below_diag = col < row
    lane = lax.broadcasted_iota(jnp.int32, (tq, LANES), 1)
    q_ext = []
    for hh in range(nh):
        tile = q_ref[0, :, (hh // 2) * LANES:(hh // 2 + 1) * LANES].astype(F32)
        q_ext.append(jnp.where(lane < hd if hh % 2 == 0 else lane >= hd, tile, 1.0).astype(BF16))

    def block_start(jj):
        return pl.multiple_of((i - jj) * tq, tq)

    def logits(jj, slot, diagonal=False):
        parts = []
        for hh in range(nh):
            y = _dot(q_ext[hh], ktb_ref[hh * LANES:(hh + 1) * LANES, pl.ds(block_start(jj), tq)])
            if diagonal:
                y = jnp.where(below_diag, y, -1e30)
            y_ref[slot, hh * tq:(hh + 1) * tq] = y
            parts.append(jnp.concatenate(_split(_softplus2(y)), axis=1))
        incl_ref[slot] = _dot(jnp.concatenate(parts, axis=0), tri)

    def weights(jj, slot):
        for hh in range(nh):
            vt = vtb_ref[hh * hd:(hh + 1) * hd, pl.ds(block_start(jj), tq)]
            incl = incl_ref[slot, hh * tq:(hh + 1) * tq]
            run = run_ref[hh]
            w = jnp.exp2(y_ref[slot, hh * tq:(hh + 1) * tq] - incl
                         - jnp.concatenate([run] * (tq // LANES), axis=1))
            acc_ref[hh] += _dot(w.astype(BF16), vt, NT_DIMS)
            run_ref[hh] = run + incl[:, :1]

    _decode_keys(dbias_ref, dtri_ref, k_pages, qb_ref, drun_ref, dw_ref)
    _decode_values(v_pages, dw_ref, dacc_ref)
    acc_ref[...] = jnp.zeros_like(acc_ref)
    run_ref[...] = jnp.zeros_like(run_ref)
    logits(0, 0, diagonal=True)

    @pl.loop(0, i // 2)
    def _(m):
        logits(2 * m + 1, 1)
        weights(2 * m, 0)
        logits(2 * m + 2, 0)
        weights(2 * m + 1, 1)

    @pl.when(i % 2 == 1)
    def _():
        logits(i, 1)
        weights(i - 1, 0)
        weights(i, 1)

    @pl.when(i % 2 == 0)
    def _():
        weights(i, 0)

    o_ref[0] = jnp.concatenate([acc_ref[hh] for hh in range(nh)], axis=-1).astype(o_ref.dtype)

    @pl.when(n % dsteps == dsteps - 1)
    def _():
        od_ref[0] = jnp.sum(dacc_ref[...], axis=-1)


def _attention(z, kt, vt, sb_bias, qt, cache_kt, cache_vt, page_table, *, d, hd, tq, nh):
    b, s, _ = z.shape
    width = nh * hd
    grid = (b, d // width, s // tq)
    r = lax.broadcasted_iota(jnp.int32, (2 * tq, tq), 0) % tq
    c = lax.broadcasted_iota(jnp.int32, (2 * tq, tq), 1)
    tri = (r >= c).astype(BF16)

    _, bs = qt.shape
    _, n_heads, _, page = cache_kt.shape
    n_pages = page_table.shape[1]
    steps = grid[0] * grid[1] * grid[2]
    assert steps % bs == 0 and n_pages % (steps // bs) == 0 and page == LANES
    dsteps = steps // bs
    npg = n_pages // dsteps
    r = lax.broadcasted_iota(jnp.int32, (page, 2 * page), 0)
    c = lax.broadcasted_iota(jnp.int32, (page, 2 * page), 1)
    dtri = jnp.logical_or(r > c, c >= page).astype(BF16)
    dbias = jnp.broadcast_to(sb_bias.astype(F32)[:, None], (n_heads, page))

    def step_of(bi, g, i):
        return (bi * grid[1] + g) * grid[2] + i

    const = lambda bi, g, i, pt: (0, 0)
    page_buffers = pltpu.VMEM((2, npg, n_heads, hd, page), F32)
    grid_spec = pltpu.PrefetchScalarGridSpec(
        num_scalar_prefetch=1,
        grid=grid,
        in_specs=[pl.BlockSpec(memory_space=pltpu.SMEM),
                  pl.BlockSpec((1, tq, width), lambda bi, g, i, pt: (bi, i, g)),
                  pl.BlockSpec((1, width, s), lambda bi, g, i, pt: (bi, g, 0)),
                  pl.BlockSpec((1, width, s), lambda bi, g, i, pt: (bi, g, 0)),
                  pl.BlockSpec((2 * tq, tq), const),
                  pl.BlockSpec((d, bs), const), pl.BlockSpec((n_heads, page), const),
                  pl.BlockSpec((page, 2 * page), const),
                  pl.BlockSpec(memory_space=pl.ANY), pl.BlockSpec(memory_space=pl.ANY)],
        out_specs=[pl.BlockSpec((1, tq, width), lambda bi, g, i, pt: (bi, i, g)),
                   pl.BlockSpec((1, n_heads, hd), lambda bi, g, i, pt: (step_of(bi, g, i) // dsteps, 0, 0))],
        scratch_shapes=[pltpu.VMEM((2, nh * tq, tq), F32), pltpu.VMEM((2, nh * tq, tq), F32),
                        pltpu.VMEM((nh, tq, hd), F32), pltpu.VMEM((nh, tq, LANES), F32),
                        pltpu.VMEM((nh * LANES, s), BF16), pltpu.VMEM((width, s), BF16),
                        pltpu.VMEM((n_heads, hd, page), F32), pltpu.VMEM((n_heads, hd, page), F32),
                        pltpu.VMEM((n_heads, page), F32), pltpu.VMEM((npg, n_heads, page), F32),
                        page_buffers, page_buffers, pltpu.SemaphoreType.DMA((2, 2))],
    )
    return pl.pallas_call(
        functools.partial(_attn_kernel, tq=tq, hd=hd, nh=nh, npg=npg, dsteps=dsteps),
        grid_spec=grid_spec,
        out_shape=[jax.ShapeDtypeStruct((b, s, d), BF16), jax.ShapeDtypeStruct((bs, n_heads, hd), F32)],
        compiler_params=_params("arbitrary", "arbitrary", "arbitrary"),
        name="attn",
    )(page_table, sb_bias.astype(F32), z, kt, vt, tri, qt, dbias, dtri, cache_kt, cache_vt)


def _ssmprep_kernel(ar_ref, ai_ref, ldt_ref, br_ref, bi_ref, abr_ref, abi_ref, bbr_ref, bbi_ref):
    dt = jnp.exp(ldt_ref[...])
    lr, li = ar_ref[...], ai_ref[...]
    mag = jnp.exp(lr * dt)
    ab_re, ab_im = mag * jnp.cos(li * dt), mag * jnp.sin(li * dt)
    den = lr * lr + li * li
    nr = ab_re - 1.0
    co_re = (nr * lr + ab_im * li) / den
    co_im = (ab_im * lr - nr * li) / den
    abr_ref[...] = ab_re
    abi_ref[...] = ab_im
    br, bi = br_ref[...], bi_ref[...]
    bbr_ref[...] = co_re * br - co_im * bi
    bbi_ref[...] = co_re * bi + co_im * br


def _ssm_discretize(a_re, a_im, log_dt, b_re, b_im):
    g, p = a_re.shape
    c = b_re.shape[-1]
    rep = lambda a: jnp.repeat(a, c, axis=0)
    rows = lambda b: b.transpose(0, 2, 1).reshape(g * c, p)
    out = jax.ShapeDtypeStruct((g * c, p), F32)
    ab_re, ab_im, bb_re, bb_im = pl.pallas_call(
        _ssmprep_kernel, out_shape=[out] * 4, name="ssmprep",
    )(rep(a_re), rep(a_im), rep(log_dt.reshape(g, 1)), rows(b_re), rows(b_im))
    return ab_re[::c], ab_im[::c], bb_re.reshape(g, c, p), bb_im.reshape(g, c, p)


def _block_diag(w, gpb):
    g, r, k = w.shape
    w = w.reshape(g // gpb, gpb, r, k)
    eye = jnp.eye(gpb, dtype=w.dtype)
    return jnp.einsum("bgrk,gh->bgrhk", w, eye).reshape(g // gpb, gpb * r, gpb * k)


def _ssm_kernel(u_ref, h0r_ref, h0i_ref, bmat_ref, ar_ref, ai_ref, cmat_ref, d_ref,
                y_ref, xr_ref, xi_ref, bu_ref, sr_ref, si_ref, *, tt, nb, ns, precise):
    t = pl.program_id(1)

    @pl.when(t == 0)
    def _():
        sr_ref[...] = h0r_ref[...]
        si_ref[...] = h0i_ref[...]

    u = u_ref[...].reshape(tt * nb, u_ref.shape[-1])
    if precise:
        bu_ref[...] = _dot3(u, bmat_ref[0])
    else:
        bu_ref[...] = _dot(u.astype(BF16), bmat_ref[0].astype(BF16))
    ar = jnp.broadcast_to(ar_ref[0], (nb, ns))
    ai = jnp.broadcast_to(ai_ref[0], (nb, ns))

    def step(k, carry):
        xr, xi = carry
        rows = pl.ds(pl.multiple_of(k * nb, nb), nb)
        nr = ar * xr - ai * xi + bu_ref[rows, :ns]
        ni = ar * xi + ai * xr + bu_ref[rows, ns:]
        bu_ref[rows, :ns] = nr
        bu_ref[rows, ns:] = ni
        return nr, ni

    xr, xi = lax.fori_loop(0, tt, step, (sr_ref[...], si_ref[...]))
    sr_ref[...] = xr
    si_ref[...] = xi
    if precise:
        y = _dot3(bu_ref[...], cmat_ref[0])
    else:
        y = _dot(bu_ref[...].astype(BF16), cmat_ref[0].astype(BF16))
    y = y + d_ref[...] * u
    y_ref[...] = y.reshape(y_ref.shape).astype(y_ref.dtype)

    @pl.when(t == pl.num_programs(1) - 1)
    def _():
        xr_ref[...] = xr
        xi_ref[...] = xi


def _ssm_scan(u, h0_re, h0_im, bmat, ab_re, ab_im, cmat, d_skip, *, tt, precise):
    s, nb, d = u.shape
    nblk, ch, ns2 = bmat.shape
    ns = ns2 // 2
    assert s % tt == 0 and d == nblk * ch
    grid = (nblk, s // tt)
    state_spec = pl.BlockSpec((nb, ns), lambda g, t: (0, g))
    return pl.pallas_call(
        functools.partial(_ssm_kernel, tt=tt, nb=nb, ns=ns, precise=precise),
        grid=grid,
        in_specs=[pl.BlockSpec((tt, nb, ch), lambda g, t: (t, 0, g)),
                  state_spec, state_spec,
                  pl.BlockSpec((1, ch, ns2), lambda g, t: (g, 0, 0)),
                  pl.BlockSpec((1, 1, ns), lambda g, t: (g, 0, 0)),
                  pl.BlockSpec((1, 1, ns), lambda g, t: (g, 0, 0)),
                  pl.BlockSpec((1, ns2, ch), lambda g, t: (g, 0, 0)),
                  pl.BlockSpec((1, ch), lambda g, t: (0, g))],
        out_specs=[pl.BlockSpec((tt, nb, ch), lambda g, t: (t, 0, g)), state_spec, state_spec],
        out_shape=[jax.ShapeDtypeStruct((s, nb, d), F32),
                   jax.ShapeDtypeStruct(h0_re.shape, F32),
                   jax.ShapeDtypeStruct(h0_re.shape, F32)],
        scratch_shapes=[pltpu.VMEM((tt * nb, ns2), F32),
                        pltpu.VMEM((nb, ns), F32), pltpu.VMEM((nb, ns), F32)],
        compiler_params=_params("parallel", "arbitrary"),
        name="ssm",
    )(u, h0_re, h0_im, bmat, ab_re, ab_im, cmat, d_skip)


def _ssm_prompt_kernel(u_ref, bmat_ref, ar_ref, ai_ref, cmat_ref, d_ref, y_ref, xr_ref, xi_ref,
                       bu_ref, sr_ref, si_ref, *, tt, nb, pitch, chunk):
    t = pl.program_id(1)

    @pl.when(t == 0)
    def _():
        sr_ref[...] = jnp.zeros_like(sr_ref)
        si_ref[...] = jnp.zeros_like(si_ref)

    nct = bu_ref.shape[0]
    half = nct // 2
    bmat = bmat_ref[0].astype(BF16)
    cmat = cmat_ref[0].astype(BF16)
    lanes = lambda a, c: a[:, c * LANES:(c + 1) * LANES]
    ar = [jnp.broadcast_to(lanes(ar_ref[0], c), (nb, LANES)) for c in range(half)]
    ai = [jnp.broadcast_to(lanes(ai_ref[0], c), (nb, LANES)) for c in range(half)]
    seq_rows = lambda b, c: slice(b * pitch + c * chunk, b * pitch + (c + 1) * chunk)

    def inputs(c):
        return jnp.concatenate([u_ref[b, c * chunk:(c + 1) * chunk, :] for b in range(nb)], axis=0)

    def fill(c):
        bu = _dot(inputs(c), bmat)
        for b in range(nb):
            for ct in range(nct):
                bu_ref[ct, seq_rows(b, c), :] = bu[b * chunk:(b + 1) * chunk, ct * LANES:(ct + 1) * LANES]

    def scan(c, xr, xi):
        for k in range(c * chunk, (c + 1) * chunk):
            rows = pl.ds(k, nb, stride=pitch)
            for ct in range(half):
                r = ar[ct] * xr[ct] - ai[ct] * xi[ct] + bu_ref[ct, rows, :]
                i = ar[ct] * xi[ct] + ai[ct] * xr[ct] + bu_ref[half + ct, rows, :]
                bu_ref[ct, rows, :] = r
                bu_ref[half + ct, rows, :] = i
                xr[ct], xi[ct] = r, i
        return xr, xi

    def readout(c):
        x = jnp.concatenate([jnp.concatenate([bu_ref[ct, seq_rows(b, c), :] for ct in range(nct)], axis=1)
                             for b in range(nb)], axis=0)
        y = _dot(x.astype(BF16), cmat) + d_ref[...] * inputs(c).astype(F32)
        for b in range(nb):
            y_ref[b, c * chunk:(c + 1) * chunk, :] = y[b * chunk:(b + 1) * chunk].astype(y_ref.dtype)

    xr = [lanes(sr_ref[...], c) for c in range(half)]
    xi = [lanes(si_ref[...], c) for c in range(half)]
    nchunks = tt // chunk
    fill(0)
    for c in range(nchunks):
        if c + 1 < nchunks:
            fill(c + 1)
        xr, xi = scan(c, xr, xi)
        if c >= 1:
            readout(c - 1)
    readout(nchunks - 1)
    xr, xi = jnp.concatenate(xr, axis=1), jnp.concatenate(xi, axis=1)
    sr_ref[...] = xr
    si_ref[...] = xi

    @pl.when(t == pl.num_programs(1) - 1)
    def _():
        xr_ref[...] = xr
        xi_ref[...] = xi


def _ssm_prompt(z, ucol, bmat, ab_re, ab_im, cmat, d_skip, *, tt):
    nb, s, _ = z.shape
    nblk, ch, ns2 = bmat.shape
    ns = ns2 // 2
    d = nblk * ch
    assert s % tt == 0 and nb % 8 == 0
    pitch = tt + 8
    state_spec = pl.BlockSpec((nb, ns), lambda g, t: (0, g))
    state_shape = jax.ShapeDtypeStruct((nb, nblk * ns), F32)
    return pl.pallas_call(
        functools.partial(_ssm_prompt_kernel, tt=tt, nb=nb, pitch=pitch, chunk=_pick(tt, 64)),
        grid=(nblk, s // tt),
        in_specs=[pl.BlockSpec((nb, tt, ch), lambda g, t: (0, t, ucol * nblk + g)),
                  pl.BlockSpec((1, ch, ns2), lambda g, t: (g, 0, 0)),
                  pl.BlockSpec((1, 1, ns), lambda g, t: (g, 0, 0)),
                  pl.BlockSpec((1, 1, ns), lambda g, t: (g, 0, 0)),
                  pl.BlockSpec((1, ns2, ch), lambda g, t: (g, 0, 0)),
                  pl.BlockSpec((1, ch), lambda g, t: (0, g))],
        out_specs=[pl.BlockSpec((nb, tt, ch), lambda g, t: (0, t, g)), state_spec, state_spec],
        out_shape=[jax.ShapeDtypeStruct((nb, s, d), BF16), state_shape, state_shape],
        scratch_shapes=[pltpu.VMEM((ns2 // LANES, nb * pitch, LANES), F32),
                        pltpu.VMEM((nb, ns), F32), pltpu.VMEM((nb, ns), F32)],
        compiler_params=_params("parallel", "arbitrary"),
        name="ssm_prompt",
    )(z, bmat, ab_re, ab_im, cmat, d_skip)


def _post_kernel(x_ref, gate_ref, y_ref, o_ref, gatt_ref, gssm_ref, matt_ref, mssm_ref,
                 wglu_ref, bglu_ref, watt_ref, wssm_ref, wout_ref, lng_ref, lnb_ref, out_ref, *, alpha):
    g = jax.nn.gelu(y_ref[0].astype(F32))
    glu = g * jax.nn.sigmoid(_dot(g.astype(BF16), wglu_ref[...]) + bglu_ref[...])
    ys = glu * jax.nn.silu(gssm_ref[0].astype(F32))
    y_ssm = _dot(ys.astype(BF16), wssm_ref[...])
    ya = o_ref[0].astype(F32) * jax.nn.silu(gatt_ref[0].astype(F32))
    y_att = _dot(ya.astype(BF16), watt_ref[...])
    merged = (jax.nn.sigmoid(matt_ref[0].astype(F32)) * y_att
              + jax.nn.sigmoid(mssm_ref[0].astype(F32)) * y_ssm)
    r = alpha * x_ref[0] + gate_ref[0] * _dot(merged.astype(BF16), wout_ref[...])
    mu = jnp.mean(r, axis=-1, keepdims=True)
    cen = r - mu
    var = jnp.mean(cen * cen, axis=-1, keepdims=True)
    out_ref[0] = cen * lax.rsqrt(var + LN_EPS) * lng_ref[...] + lnb_ref[...]


def _post(x, gate, y, o_att, z, cols, w_glu, b_glu, w_att_out, w_ssm_out, w_out, ln_g, ln_b, *, tm, alpha):
    b, s, d = x.shape
    per_row = gate.shape[1] != 1
    row = lambda col: pl.BlockSpec((1, tm, d), lambda bi, m, col=col: (bi, m, col))
    gate_spec = row(0) if per_row else pl.BlockSpec((1, 1, d), lambda bi, m: (bi, 0, 0))
    mat = pl.BlockSpec((d, d), lambda bi, m: (0, 0))
    vec = pl.BlockSpec((1, d), lambda bi, m: (0, 0))
    return pl.pallas_call(
        functools.partial(_post_kernel, alpha=alpha),
        grid=(b, s // tm),
        in_specs=[row(0), gate_spec, row(0), row(0)] + [row(c) for c in cols]
                 + [mat, vec, mat, mat, mat, vec, vec],
        out_specs=row(0),
        out_shape=jax.ShapeDtypeStruct((b, s, d), F32),
        compiler_params=_params("parallel", "parallel"),
        name="post",
    )(x, gate, y, o_att, z, z, z, z,
      w_glu.astype(BF16), b_glu.reshape(1, d), w_att_out.astype(BF16), w_ssm_out.astype(BF16),
      w_out.astype(BF16), ln_g.reshape(1, d), ln_b.reshape(1, d))


def _pick(n, target):
    t = min(n, target)
    while n % t:
        t //= 2
    return t


def kernel(x_prompt, x_sample, c_prompt, c_sample, cache_k, cache_v, state_ssm_re, state_ssm_im, page_table, w_cond, b_cond, w_in, sb_bias, ssm_a_re, ssm_a_im, ssm_log_dt, ssm_b_re, ssm_b_im, ssm_c_re, ssm_c_im, ssm_d, w_glu, b_glu, w_att_out, w_ssm_out, w_out, ln_g, ln_b):
    depth = w_in.shape[0]
    assert depth == 1, "single-layer trunk"
    bsz, seq, d = x_prompt.shape
    dbsz, dseq, _ = x_sample.shape
    assert dseq == 1 and w_in.shape[2] == 8 * d
    n_heads = sb_bias.shape[-1]
    hd = cache_k.shape[-1]
    n_groups, n_state = ssm_a_re.shape[1:]
    alpha = (2.0 * depth) ** 0.25
    l = 0

    mod = _modulation(jnp.concatenate([c_prompt, c_sample], axis=0), w_cond[l], b_cond[l])
    shift, scale, gate = mod[:, :d], mod[:, d:2 * d], mod[:, 2 * d:]
    row3 = lambda a, lo, hi, shape: a[lo:hi].reshape(shape)
    p_shape, s_shape = (bsz, 1, d), (1, dbsz, d)

    ab_re, ab_im, bb_re, bb_im = _ssm_discretize(ssm_a_re[l], ssm_a_im[l], ssm_log_dt[l],
                                                 ssm_b_re[l], ssm_b_im[l])
    gpb = GROUPS_PER_BLOCK
    nblk = n_groups // gpb
    bmat = jnp.concatenate([_block_diag(bb_re, gpb), _block_diag(bb_im, gpb)], axis=-1)
    cmat = jnp.concatenate([_block_diag(ssm_c_re[l].transpose(0, 2, 1), gpb),
                            _block_diag(-ssm_c_im[l].transpose(0, 2, 1), gpb)], axis=1)
    a_blk = lambda a: a.reshape(nblk, 1, gpb * n_state)
    d_skip = ssm_d[l].reshape(1, d)

    w = w_in[l]
    wt = w[:, :3 * d].reshape(d, 3, d).transpose(1, 2, 0)
    wt_kvq = jnp.stack([wt[1], wt[2], wt[0]])

    kt_p, vt_p, z_p = _inproj_prompt(x_prompt, row3(scale, 0, bsz, p_shape), row3(shift, 0, bsz, p_shape),
                                     w.astype(BF16), wt_kvq[:2].astype(BF16),
                                     tm=_pick(seq, 512), cols=(0, 3, 4, 5, 6, 7))
    x_s = x_sample.reshape(dbsz, d)
    kvqt_s, z_s = _inproj_sample(x_s, scale[bsz:], shift[bsz:], w, wt_kvq)
    z_s = z_s.reshape(1, dbsz, 5 * d)

    pool = lambda c: c[l].transpose(0, 2, 3, 1)
    o_p, o_s = _attention(z_p, kt_p, vt_p, sb_bias[l], kvqt_s[2], pool(cache_k), pool(cache_v), page_table,
                          d=d, hd=hd, tq=_pick(seq, 256), nh=4)

    y_p, sr_p, si_p = _ssm_prompt(z_p, 2, bmat, a_blk(ab_re), a_blk(ab_im), cmat, d_skip, tt=_pick(seq, 256))
    out_p = _post(x_prompt, row3(gate, 0, bsz, p_shape), y_p, o_p, z_p, (1, 3, 4, 5),
                  w_glu[l], b_glu[l], w_att_out[l], w_ssm_out[l], w_out[l], ln_g[l], ln_b[l],
                  tm=_pick(seq, 512), alpha=alpha)
    y_s, sr_s, si_s = _ssm_scan(z_s[:, :, d:2 * d], state_ssm_re[l].reshape(dbsz, -1),
                                state_ssm_im[l].reshape(dbsz, -1), bmat, a_blk(ab_re), a_blk(ab_im),
                                cmat, d_skip, tt=1, precise=True)
    out_s = _post(x_s.reshape(1, dbsz, d), row3(gate, bsz, bsz + dbsz, s_shape), y_s, o_s.reshape(1, dbsz, d),
                  z_s, (0, 2, 3, 4),
                  w_glu[l], b_glu[l], w_att_out[l], w_ssm_out[l], w_out[l], ln_g[l], ln_b[l],
                  tm=dbsz, alpha=alpha)

    heads_t = lambda t, b, s: t.reshape(1, b, n_heads, hd, s).transpose(0, 1, 4, 2, 3)
    state = lambda t, b: t.reshape(1, b, n_groups, n_state)
    return (out_p, out_s.reshape(dbsz, 1, d),
            heads_t(kt_p, bsz, seq), heads_t(vt_p, bsz, seq), state(sr_p, bsz), state(si_p, bsz),
            heads_t(kvqt_s[0], 1, dbsz).reshape(1, dbsz, 1, n_heads, hd),
            heads_t(kvqt_s[1], 1, dbsz).reshape(1, dbsz, 1, n_heads, hd),
            state(sr_s, dbsz), state(si_s, dbsz))
```

```python
import functools
import math

import jax
import jax.numpy as jnp
from jax import lax
from jax.experimental import pallas as pl
from jax.experimental.pallas import tpu as pltpu

F32 = jnp.float32
BF16 = jnp.bfloat16
LN_EPS = 1e-5
LOG2E = math.log2(math.e)
LANES = 128
GROUPS_PER_BLOCK = 8
VMEM_LIMIT = 56 * 1024 * 1024
NT_DIMS = (((1,), (1,)), ((), ()))


def _dot(a, b, dims=None):
    if dims is None:
        return jnp.dot(a, b, preferred_element_type=F32)
    return lax.dot_general(a, b, dims, preferred_element_type=F32)


def _split(a):
    hi = a.astype(BF16)
    lo = (a - hi.astype(F32)).astype(BF16)
    return hi, lo


def _dot3(a, b, dims=None):
    ah, al = _split(a)
    bh, bl = _split(b)
    return _dot(ah, bh, dims) + (_dot(ah, bl, dims) + _dot(al, bh, dims))


def _dot_hilo(a, b_bf16):
    hi, lo = _split(a)
    return _dot(hi, b_bf16) + _dot(lo, b_bf16)


def _softplus2(y):
    return jnp.maximum(y, 0.0) + jnp.log2(1.0 + jnp.exp2(-jnp.abs(y)))


def _params(*sem):
    return pltpu.CompilerParams(dimension_semantics=sem, vmem_limit_bytes=VMEM_LIMIT)


def _mod_kernel(c_ref, w_ref, b_ref, o_ref):
    o_ref[...] = _dot3(c_ref[...], w_ref[...]) + b_ref[...]


def _modulation(c, w_cond, b_cond):
    rows, d = c.shape
    n = w_cond.shape[1]
    tn = 768 if n % 768 == 0 else n
    return pl.pallas_call(
        _mod_kernel,
        grid=(n // tn,),
        in_specs=[pl.BlockSpec((rows, d), lambda j: (0, 0)),
                  pl.BlockSpec((d, tn), lambda j: (0, j)),
                  pl.BlockSpec((1, tn), lambda j: (0, j))],
        out_specs=pl.BlockSpec((rows, tn), lambda j: (0, j)),
        out_shape=jax.ShapeDtypeStruct((rows, n), F32),
        compiler_params=_params("arbitrary"),
        name="mod",
    )(c, w_cond, b_cond.reshape(1, n))


def _inproj_prompt_kernel(x_ref, scale_ref, shift_ref, w_ref, wt_ref, kt_ref, vt_ref, z_ref, *, d, cols):
    h = (x_ref[0] * (1.0 + scale_ref[0]) + shift_ref[0]).astype(BF16)
    kt_ref[0] = _dot(wt_ref[0], h, NT_DIMS)
    vt_ref[0] = _dot(wt_ref[1], h, NT_DIMS)
    for gi, c in enumerate(cols):
        z_ref[0, :, gi * d:(gi + 1) * d] = _dot(h, w_ref[:, c * d:(c + 1) * d]).astype(z_ref.dtype)


def _inproj_prompt(x, scale, shift, w, wt, *, tm, cols):
    b, s, d = x.shape
    assert s % tm == 0
    resident = lambda shape: pl.BlockSpec(shape, lambda bi, m: (0,) * len(shape),
                                          pipeline_mode=pl.Buffered(1))
    mod_spec = pl.BlockSpec((1, 1, d), lambda bi, m: (bi, 0, 0))
    t_spec = pl.BlockSpec((1, d, tm), lambda bi, m: (bi, 0, m))
    return pl.pallas_call(
        functools.partial(_inproj_prompt_kernel, d=d, cols=cols),
        grid=(b, s // tm),
        in_specs=[pl.BlockSpec((1, tm, d), lambda bi, m: (bi, m, 0)), mod_spec, mod_spec,
                  resident(w.shape), resident(wt.shape)],
        out_specs=[t_spec, t_spec, pl.BlockSpec((1, tm, len(cols) * d), lambda bi, m: (bi, m, 0))],
        out_shape=[jax.ShapeDtypeStruct((b, d, s), F32), jax.ShapeDtypeStruct((b, d, s), F32),
                   jax.ShapeDtypeStruct((b, s, len(cols) * d), BF16)],
        compiler_params=_params("parallel", "parallel"),
        name="inproj_prompt",
    )(x, scale, shift, w, wt)


def _inproj_sample_kernel(x_ref, scale_ref, shift_ref, w_ref, wt_ref, t_ref, z_ref, h_ref, *, nt):
    j = pl.program_id(0)

    @pl.when(j == 0)
    def _():
        h_ref[...] = x_ref[...] * (1.0 + scale_ref[...]) + shift_ref[...]

    @pl.when(j < nt)
    def _():
        t_ref[0] = _dot3(wt_ref[0], h_ref[...], NT_DIMS)

    @pl.when(j >= nt)
    def _():
        z_ref[...] = _dot3(h_ref[...], w_ref[...])


def _inproj_sample(x, scale, shift, w, wt):
    r, d = x.shape
    ng, nt = w.shape[1] // d, wt.shape[0]
    rows = pl.BlockSpec((r, d), lambda j: (0, 0))
    return pl.pallas_call(
        functools.partial(_inproj_sample_kernel, nt=nt),
        grid=(ng,),
        in_specs=[rows, rows, rows,
                  pl.BlockSpec((d, d), lambda j: (0, jnp.maximum(j, nt))),
                  pl.BlockSpec((1, d, d), lambda j: (jnp.minimum(j, nt - 1), 0, 0))],
        out_specs=[pl.BlockSpec((1, d, r), lambda j: (jnp.minimum(j, nt - 1), 0, 0)),
                   pl.BlockSpec((r, d), lambda j: (0, jnp.maximum(j - nt, 0)))],
        out_shape=[jax.ShapeDtypeStruct((nt, d, r), F32),
                   jax.ShapeDtypeStruct((r, (ng - nt) * d), F32)],
        scratch_shapes=[pltpu.VMEM((r, d), F32)],
        compiler_params=_params("arbitrary"),
        name="inproj_sample",
    )(x, scale, shift, w, wt)


def _decode_begin(seq, qt_ref, qb_ref, acc_ref, run_ref, *, hd):
    nb, page = qt_ref.shape[1], acc_ref.shape[-1]
    onehot = (lax.broadcasted_iota(jnp.int32, (nb, page), 0) == seq).astype(BF16)
    qb = _dot_hilo(qt_ref[...], onehot) * (hd ** -0.5 * LOG2E)
    qb_ref[...] = qb.reshape(qb_ref.shape)
    acc_ref[...] = jnp.zeros_like(acc_ref)
    run_ref[...] = jnp.zeros_like(run_ref)


def _decode_keys(bias_ref, tri_ref, k_refs, qb_ref, run_ref, w_ref):
    page = run_ref.shape[-1]
    qb = qb_ref[...]
    bias2 = bias_ref[...] * LOG2E
    tri = tri_ref[...]
    run = run_ref[...]
    for k, kr in enumerate(k_refs):
        y = jnp.sum(kr[...] * qb, axis=1) + bias2
        sp = _softplus2(y)
        ext = _dot_hilo(sp, tri)
        w_ref[k] = jnp.exp2(y - sp - ext[:, :page] - run)
        run = run + ext[:, page:]
    run_ref[...] = run


def _decode_values(v_refs, w_ref, acc_ref):
    weights = [w_ref[k] for k in range(len(v_refs))]
    for h in range(acc_ref.shape[0]):
        part = v_refs[0][h] * weights[0][h:h + 1, :]
        for k in range(1, len(v_refs)):
            part = part + v_refs[k][h] * weights[k][h:h + 1, :]
        acc_ref[h] += part


def _attn_kernel(*refs, nqb, **kw):
    for t in range(nqb):
        _attn_block(t, nqb, *refs, **kw)


def _attn_block(t, nqb, pt_ref, bias_ref, q_ref, kt_ref, vt_ref, tri_ref, qt_ref, dbias_ref, dtri_ref,
                ck_ref, cv_ref, o_ref, od_ref, y_ref, incl_ref, acc_ref, run_ref, ktb_ref, vtb_ref,
                qb_ref, dacc_ref, drun_ref, dw_ref, kbuf_ref, vbuf_ref, sem_ref, *, tq, hd, nh, npg, dsteps):
    g = pl.program_id(1)
    i = pl.program_id(2) * nqb + t
    n = (pl.program_id(0) * pl.num_programs(1) + g) * pl.num_programs(2) * nqb + i
    steps = pl.num_programs(0) * pl.num_programs(1) * pl.num_programs(2) * nqb
    q_rows = slice(t * tq, (t + 1) * tq)
    slot = n % 2
    assert 2 * hd == LANES and nh % 2 == 0

    def page_copies(step, slot):
        seq, first = step // dsteps, (step % dsteps) * npg
        n_pages = pt_ref.shape[1]
        copies = []
        for k in range(npg):
            page = pt_ref[seq, n_pages - 1 - (first + k)]
            copies.append(pltpu.make_async_copy(ck_ref.at[page], kbuf_ref.at[slot, k], sem_ref.at[slot, 0]))
            copies.append(pltpu.make_async_copy(cv_ref.at[page], vbuf_ref.at[slot, k], sem_ref.at[slot, 1]))
        return copies

    @pl.when(n == 0)
    def _():
        for c in page_copies(0, 0):
            c.start()

    @pl.when(n + 1 < steps)
    def _():
        for c in page_copies(n + 1, 1 - slot):
            c.start()

    for c in page_copies(n, slot):
        c.wait()
    k_pages = [kbuf_ref.at[slot, k] for k in range(npg)]
    v_pages = [vbuf_ref.at[slot, k] for k in range(npg)]

    @pl.when(n % dsteps == 0)
    def _():
        _decode_begin(n // dsteps, qt_ref, qb_ref, dacc_ref, drun_ref, hd=hd)

    @pl.when(i == 0)
    def _():
        s = kt_ref.shape[-1]
        r = lax.broadcasted_iota(jnp.int32, (hd, s), 0)
        for hh in range(nh):
            rows = slice(hh * hd, (hh + 1) * hd)
            keys = (kt_ref[0, rows, :] * (hd ** -0.5 * LOG2E)).astype(BF16)
            bias2 = jnp.full((hd, s), bias_ref[g * nh + hh] * LOG2E, F32)
            b_hi = bias2.astype(BF16).astype(F32)
            bias_rows = jnp.where(r == 0, b_hi, jnp.where(r == 1, bias2 - b_hi, 0.0)).astype(BF16)
            pair = [keys, bias_rows] if hh % 2 == 0 else [bias_rows, keys]
            ktb_ref[hh * LANES:(hh + 1) * LANES, :] = jnp.concatenate(pair, axis=0)
        vtb_ref[...] = vt_ref[0].astype(BF16)

    tri = tri_ref[...]
    row = lax.broadcasted_iota(jnp.int32, (tq, tq), 0)
    col = lax.broadcasted_iota(jnp.int32, (tq, tq), 1)
    below_diag = col < row
    lane = lax.broadcasted_iota(jnp.int32, (tq, LANES), 1)
    q_ext = []
    for hh in range(nh):
        tile = q_ref[0, q_rows, (hh // 2) * LANES:(hh // 2 + 1) * LANES].astype(F32)
        q_ext.append(jnp.where(lane < hd if hh % 2 == 0 else lane >= hd, tile, 1.0).astype(BF16))

    def block_start(jj):
        return pl.multiple_of((i - jj) * tq, tq)

    def logits(jj, slot, diagonal=False):
        parts = []
        for hh in range(nh):
            y = _dot(q_ext[hh], ktb_ref[hh * LANES:(hh + 1) * LANES, pl.ds(block_start(jj), tq)])
            if diagonal:
                y = jnp.where(below_diag, y, -1e30)
            y_ref[slot, hh * tq:(hh + 1) * tq] = y
            parts.append(jnp.concatenate(_split(_softplus2(y)), axis=1))
        incl_ref[slot] = _dot(jnp.concatenate(parts, axis=0), tri)

    def weights(jj, slot):
        for hh in range(nh):
            vt = vtb_ref[hh * hd:(hh + 1) * hd, pl.ds(block_start(jj), tq)]
            incl = incl_ref[slot, hh * tq:(hh + 1) * tq]
            run = run_ref[hh]
            w = jnp.exp2(y_ref[slot, hh * tq:(hh + 1) * tq] - incl
                         - jnp.concatenate([run] * (tq // LANES), axis=1))
            acc_ref[hh] += _dot(w.astype(BF16), vt, NT_DIMS)
            run_ref[hh] = run + incl[:, :1]

    _decode_keys(dbias_ref, dtri_ref, k_pages, qb_ref, drun_ref, dw_ref)
    _decode_values(v_pages, dw_ref, dacc_ref)
    acc_ref[...] = jnp.zeros_like(acc_ref)
    run_ref[...] = jnp.zeros_like(run_ref)
    logits(0, 0, diagonal=True)

    @pl.loop(0, i // 2)
    def _(m):
        logits(2 * m + 1, 1)
        weights(2 * m, 0)
        logits(2 * m + 2, 0)
        weights(2 * m + 1, 1)

    @pl.when(i % 2 == 1)
    def _():
        logits(i, 1)
        weights(i - 1, 0)
        weights(i, 1)

    @pl.when(i % 2 == 0)
    def _():
        weights(i, 0)

    o_ref[0, q_rows, :] = jnp.concatenate([acc_ref[hh] for hh in range(nh)], axis=-1).astype(o_ref.dtype)

    @pl.when(n % dsteps == dsteps - 1)
    def _():
        od_ref[(n // dsteps) % od_ref.shape[0]] = jnp.sum(dacc_ref[...], axis=-1)


def _attention(z, kt, vt, sb_bias, qt, cache_kt, cache_vt, page_table, *, d, hd, tq, nh):
    b, s, _ = z.shape
    width = nh * hd
    nqb = 2 if (s // tq) % 2 == 0 else 1
    grid = (b, d // width, s // (tq * nqb))
    r = lax.broadcasted_iota(jnp.int32, (2 * tq, tq), 0) % tq
    c = lax.broadcasted_iota(jnp.int32, (2 * tq, tq), 1)
    tri = (r >= c).astype(BF16)

    _, bs = qt.shape
    _, n_heads, _, page = cache_kt.shape
    n_pages = page_table.shape[1]
    steps = grid[0] * grid[1] * grid[2] * nqb
    assert steps % bs == 0 and n_pages % (steps // bs) == 0 and page == LANES
    dsteps = steps // bs
    npg = n_pages // dsteps
    spb = max(1, nqb // dsteps)
    assert nqb % dsteps == 0 or dsteps % nqb == 0
    r = lax.broadcasted_iota(jnp.int32, (page, 2 * page), 0)
    c = lax.broadcasted_iota(jnp.int32, (page, 2 * page), 1)
    dtri = jnp.logical_or(r > c, c >= page).astype(BF16)
    dbias = jnp.broadcast_to(sb_bias.astype(F32)[:, None], (n_heads, page))

    def step_of(bi, g, i):
        return ((bi * grid[1] + g) * grid[2] + i) * nqb

    const = lambda bi, g, i, pt: (0, 0)
    page_buffers = pltpu.VMEM((2, npg, n_heads, hd, page), F32)
    grid_spec = pltpu.PrefetchScalarGridSpec(
        num_scalar_prefetch=1,
        grid=grid,
        in_specs=[pl.BlockSpec(memory_space=pltpu.SMEM),
                  pl.BlockSpec((1, tq * nqb, width), lambda bi, g, i, pt: (bi, i, g)),
                  pl.BlockSpec((1, width, s), lambda bi, g, i, pt: (bi, g, 0)),
                  pl.BlockSpec((1, width, s), lambda bi, g, i, pt: (bi, g, 0)),
                  pl.BlockSpec((2 * tq, tq), const),
                  pl.BlockSpec((d, bs), const), pl.BlockSpec((n_heads, page), const),
                  pl.BlockSpec((page, 2 * page), const),
                  pl.BlockSpec(memory_space=pl.ANY), pl.BlockSpec(memory_space=pl.ANY)],
        out_specs=[pl.BlockSpec((1, tq * nqb, width), lambda bi, g, i, pt: (bi, i, g)),
                   pl.BlockSpec((spb, n_heads, hd),
                                lambda bi, g, i, pt: (step_of(bi, g, i) // (dsteps * spb), 0, 0))],
        scratch_shapes=[pltpu.VMEM((2, nh * tq, tq), F32), pltpu.VMEM((2, nh * tq, tq), F32),
                        pltpu.VMEM((nh, tq, hd), F32), pltpu.VMEM((nh, tq, LANES), F32),
                        pltpu.VMEM((nh * LANES, s), BF16), pltpu.VMEM((width, s), BF16),
                        pltpu.VMEM((n_heads, hd, page), F32), pltpu.VMEM((n_heads, hd, page), F32),
                        pltpu.VMEM((n_heads, page), F32), pltpu.VMEM((npg, n_heads, page), F32),
                        page_buffers, page_buffers, pltpu.SemaphoreType.DMA((2, 2))],
    )
    return pl.pallas_call(
        functools.partial(_attn_kernel, nqb=nqb, tq=tq, hd=hd, nh=nh, npg=npg, dsteps=dsteps),
        grid_spec=grid_spec,
        out_shape=[jax.ShapeDtypeStruct((b, s, d), BF16), jax.ShapeDtypeStruct((bs, n_heads, hd), F32)],
        compiler_params=_params("arbitrary", "arbitrary", "arbitrary"),
        name="attn",
    )(page_table, sb_bias.astype(F32), z, kt, vt, tri, qt, dbias, dtri, cache_kt, cache_vt)


def _ssmprep_kernel(ar_ref, ai_ref, ldt_ref, br_ref, bi_ref, abr_ref, abi_ref, bbr_ref, bbi_ref):
    dt = jnp.exp(ldt_ref[...])
    lr, li = ar_ref[...], ai_ref[...]
    mag = jnp.exp(lr * dt)
    ab_re, ab_im = mag * jnp.cos(li * dt), mag * jnp.sin(li * dt)
    den = lr * lr + li * li
    nr = ab_re - 1.0
    co_re = (nr * lr + ab_im * li) / den
    co_im = (ab_im * lr - nr * li) / den
    abr_ref[...] = ab_re
    abi_ref[...] = ab_im
    br, bi = br_ref[...], bi_ref[...]
    bbr_ref[...] = co_re * br - co_im * bi
    bbi_ref[...] = co_re * bi + co_im * br


def _ssm_discretize(a_re, a_im, log_dt, b_re, b_im):
    g, p = a_re.shape
    c = b_re.shape[-1]
    rep = lambda a: jnp.repeat(a, c, axis=0)
    rows = lambda b: b.transpose(0, 2, 1).reshape(g * c, p)
    out = jax.ShapeDtypeStruct((g * c, p), F32)
    ab_re, ab_im, bb_re, bb_im = pl.pallas_call(
        _ssmprep_kernel, out_shape=[out] * 4, name="ssmprep",
    )(rep(a_re), rep(a_im), rep(log_dt.reshape(g, 1)), rows(b_re), rows(b_im))
    return ab_re[::c], ab_im[::c], bb_re.reshape(g, c, p), bb_im.reshape(g, c, p)


def _block_diag(w, gpb):
    g, r, k = w.shape
    w = w.reshape(g // gpb, gpb, r, k)
    eye = jnp.eye(gpb, dtype=w.dtype)
    return jnp.einsum("bgrk,gh->bgrhk", w, eye).reshape(g // gpb, gpb * r, gpb * k)


def _ssm_kernel(u_ref, h0r_ref, h0i_ref, bmat_ref, ar_ref, ai_ref, cmat_ref, d_ref,
                y_ref, xr_ref, xi_ref, bu_ref, sr_ref, si_ref, *, tt, nb, ns, precise):
    t = pl.program_id(1)

    @pl.when(t == 0)
    def _():
        sr_ref[...] = h0r_ref[...]
        si_ref[...] = h0i_ref[...]

    u = u_ref[...].reshape(tt * nb, u_ref.shape[-1])
    if precise:
        bu_ref[...] = _dot3(u, bmat_ref[0])
    else:
        bu_ref[...] = _dot(u.astype(BF16), bmat_ref[0].astype(BF16))
    ar = jnp.broadcast_to(ar_ref[0], (nb, ns))
    ai = jnp.broadcast_to(ai_ref[0], (nb, ns))

    def step(k, carry):
        xr, xi = carry
        rows = pl.ds(pl.multiple_of(k * nb, nb), nb)
        nr = ar * xr - ai * xi + bu_ref[rows, :ns]
        ni = ar * xi + ai * xr + bu_ref[rows, ns:]
        bu_ref[rows, :ns] = nr
        bu_ref[rows, ns:] = ni
        return nr, ni

    xr, xi = lax.fori_loop(0, tt, step, (sr_ref[...], si_ref[...]))
    sr_ref[...] = xr
    si_ref[...] = xi
    if precise:
        y = _dot3(bu_ref[...], cmat_ref[0])
    else:
        y = _dot(bu_ref[...].astype(BF16), cmat_ref[0].astype(BF16))
    y = y + d_ref[...] * u
    y_ref[...] = y.reshape(y_ref.shape).astype(y_ref.dtype)

    @pl.when(t == pl.num_programs(1) - 1)
    def _():
        xr_ref[...] = xr
        xi_ref[...] = xi


def _ssm_scan(u, h0_re, h0_im, bmat, ab_re, ab_im, cmat, d_skip, *, tt, precise):
    s, nb, d = u.shape
    nblk, ch, ns2 = bmat.shape
    ns = ns2 // 2
    assert s % tt == 0 and d == nblk * ch
    grid = (nblk, s // tt)
    state_spec = pl.BlockSpec((nb, ns), lambda g, t: (0, g))
    return pl.pallas_call(
        functools.partial(_ssm_kernel, tt=tt, nb=nb, ns=ns, precise=precise),
        grid=grid,
        in_specs=[pl.BlockSpec((tt, nb, ch), lambda g, t: (t, 0, g)),
                  state_spec, state_spec,
                  pl.BlockSpec((1, ch, ns2), lambda g, t: (g, 0, 0)),
                  pl.BlockSpec((1, 1, ns), lambda g, t: (g, 0, 0)),
                  pl.BlockSpec((1, 1, ns), lambda g, t: (g, 0, 0)),
                  pl.BlockSpec((1, ns2, ch), lambda g, t: (g, 0, 0)),
                  pl.BlockSpec((1, ch), lambda g, t: (0, g))],
        out_specs=[pl.BlockSpec((tt, nb, ch), lambda g, t: (t, 0, g)), state_spec, state_spec],
        out_shape=[jax.ShapeDtypeStruct((s, nb, d), F32),
                   jax.ShapeDtypeStruct(h0_re.shape, F32),
                   jax.ShapeDtypeStruct(h0_re.shape, F32)],
        scratch_shapes=[pltpu.VMEM((tt * nb, ns2), F32),
                        pltpu.VMEM((nb, ns), F32), pltpu.VMEM((nb, ns), F32)],
        compiler_params=_params("parallel", "arbitrary"),
        name="ssm",
    )(u, h0_re, h0_im, bmat, ab_re, ab_im, cmat, d_skip)


def _ssm_prompt_kernel(u_ref, bmat_ref, ar_ref, ai_ref, cmat_ref, d_ref, y_ref, xr_ref, xi_ref,
                       bu_ref, sr_ref, si_ref, *, tt, nb, pitch, chunk):
    t = pl.program_id(1)

    @pl.when(t == 0)
    def _():
        sr_ref[...] = jnp.zeros_like(sr_ref)
        si_ref[...] = jnp.zeros_like(si_ref)

    nct = bu_ref.shape[0]
    half = nct // 2
    bmat = bmat_ref[0].astype(BF16)
    cmat = cmat_ref[0].astype(BF16)
    lanes = lambda a, c: a[:, c * LANES:(c + 1) * LANES]
    ar = [jnp.broadcast_to(lanes(ar_ref[0], c), (nb, LANES)) for c in range(half)]
    ai = [jnp.broadcast_to(lanes(ai_ref[0], c), (nb, LANES)) for c in range(half)]
    seq_rows = lambda b, c: slice(b * pitch + c * chunk, b * pitch + (c + 1) * chunk)

    def inputs(c):
        return jnp.concatenate([u_ref[b, c * chunk:(c + 1) * chunk, :] for b in range(nb)], axis=0)

    def fill(c):
        bu = _dot(inputs(c), bmat)
        for b in range(nb):
            for ct in range(nct):
                bu_ref[ct, seq_rows(b, c), :] = bu[b * chunk:(b + 1) * chunk, ct * LANES:(ct + 1) * LANES]

    def scan(c, xr, xi):
        for k in range(c * chunk, (c + 1) * chunk):
            rows = pl.ds(k, nb, stride=pitch)
            for ct in range(half):
                r = ar[ct] * xr[ct] - ai[ct] * xi[ct] + bu_ref[ct, rows, :]
                i = ar[ct] * xi[ct] + ai[ct] * xr[ct] + bu_ref[half + ct, rows, :]
                bu_ref[ct, rows, :] = r
                bu_ref[half + ct, rows, :] = i
                xr[ct], xi[ct] = r, i
        return xr, xi

    def readout(c):
        x = jnp.concatenate([jnp.concatenate([bu_ref[ct, seq_rows(b, c), :] for ct in range(nct)], axis=1)
                             for b in range(nb)], axis=0)
        y = _dot(x.astype(BF16), cmat) + d_ref[...] * inputs(c).astype(F32)
        for b in range(nb):
            y_ref[b, c * chunk:(c + 1) * chunk, :] = y[b * chunk:(b + 1) * chunk].astype(y_ref.dtype)

    xr = [lanes(sr_ref[...], c) for c in range(half)]
    xi = [lanes(si_ref[...], c) for c in range(half)]
    nchunks = tt // chunk
    fill(0)
    for c in range(nchunks):
        if c + 1 < nchunks:
            fill(c + 1)
        xr, xi = scan(c, xr, xi)
        if c >= 1:
            readout(c - 1)
    readout(nchunks - 1)
    xr, xi = jnp.concatenate(xr, axis=1), jnp.concatenate(xi, axis=1)
    sr_ref[...] = xr
    si_ref[...] = xi

    @pl.when(t == pl.num_programs(1) - 1)
    def _():
        xr_ref[...] = xr
        xi_ref[...] = xi


def _ssm_prompt(z, ucol, bmat, ab_re, ab_im, cmat, d_skip, *, tt):
    nb, s, _ = z.shape
    nblk, ch, ns2 = bmat.shape
    ns = ns2 // 2
    d = nblk * ch
    assert s % tt == 0 and nb % 8 == 0
    pitch = tt + 8
    state_spec = pl.BlockSpec((nb, ns), lambda g, t: (0, g))
    state_shape = jax.ShapeDtypeStruct((nb, nblk * ns), F32)
    return pl.pallas_call(
        functools.partial(_ssm_prompt_kernel, tt=tt, nb=nb, pitch=pitch, chunk=_pick(tt, 64)),
        grid=(nblk, s // tt),
        in_specs=[pl.BlockSpec((nb, tt, ch), lambda g, t: (0, t, ucol * nblk + g)),
                  pl.BlockSpec((1, ch, ns2), lambda g, t: (g, 0, 0)),
                  pl.BlockSpec((1, 1, ns), lambda g, t: (g, 0, 0)),
                  pl.BlockSpec((1, 1, ns), lambda g, t: (g, 0, 0)),
                  pl.BlockSpec((1, ns2, ch), lambda g, t: (g, 0, 0)),
                  pl.BlockSpec((1, ch), lambda g, t: (0, g))],
        out_specs=[pl.BlockSpec((nb, tt, ch), lambda g, t: (0, t, g)), state_spec, state_spec],
        out_shape=[jax.ShapeDtypeStruct((nb, s, d), BF16), state_shape, state_shape],
        scratch_shapes=[pltpu.VMEM((ns2 // LANES, nb * pitch, LANES), F32),
                        pltpu.VMEM((nb, ns), F32), pltpu.VMEM((nb, ns), F32)],
        compiler_params=_params("parallel", "arbitrary"),
        name="ssm_prompt",
    )(z, bmat, ab_re, ab_im, cmat, d_skip)


def _post_kernel(x_ref, gate_ref, y_ref, o_ref, gatt_ref, gssm_ref, matt_ref, mssm_ref,
                 wglu_ref, bglu_ref, watt_ref, wssm_ref, wout_ref, lng_ref, lnb_ref, out_ref, *, alpha):
    g = jax.nn.gelu(y_ref[0].astype(F32))
    glu = g * jax.nn.sigmoid(_dot(g.astype(BF16), wglu_ref[...]) + bglu_ref[...])
    ys = glu * jax.nn.silu(gssm_ref[0].astype(F32))
    y_ssm = _dot(ys.astype(BF16), wssm_ref[...])
    ya = o_ref[0].astype(F32) * jax.nn.silu(gatt_ref[0].astype(F32))
    y_att = _dot(ya.astype(BF16), watt_ref[...])
    merged = (jax.nn.sigmoid(matt_ref[0].astype(F32)) * y_att
              + jax.nn.sigmoid(mssm_ref[0].astype(F32)) * y_ssm)
    r = alpha * x_ref[0] + gate_ref[0] * _dot(merged.astype(BF16), wout_ref[...])
    mu = jnp.mean(r, axis=-1, keepdims=True)
    cen = r - mu
    var = jnp.mean(cen * cen, axis=-1, keepdims=True)
    out_ref[0] = cen * lax.rsqrt(var + LN_EPS) * lng_ref[...] + lnb_ref[...]


def _post(x, gate, y, o_att, z, cols, w_glu, b_glu, w_att_out, w_ssm_out, w_out, ln_g, ln_b, *, tm, alpha):
    b, s, d = x.shape
    per_row = gate.shape[1] != 1
    row = lambda col: pl.BlockSpec((1, tm, d), lambda bi, m, col=col: (bi, m, col))
    gate_spec = row(0) if per_row else pl.BlockSpec((1, 1, d), lambda bi, m: (bi, 0, 0))
    mat = pl.BlockSpec((d, d), lambda bi, m: (0, 0))
    vec = pl.BlockSpec((1, d), lambda bi, m: (0, 0))
    return pl.pallas_call(
        functools.partial(_post_kernel, alpha=alpha),
        grid=(b, s // tm),
        in_specs=[row(0), gate_spec, row(0), row(0)] + [row(c) for c in cols]
                 + [mat, vec, mat, mat, mat, vec, vec],
        out_specs=row(0),
        out_shape=jax.ShapeDtypeStruct((b, s, d), F32),
        compiler_params=_params("parallel", "parallel"),
        name="post",
    )(x, gate, y, o_att, z, z, z, z,
      w_glu.astype(BF16), b_glu.reshape(1, d), w_att_out.astype(BF16), w_ssm_out.astype(BF16),
      w_out.astype(BF16), ln_g.reshape(1, d), ln_b.reshape(1, d))


def _pick(n, target):
    t = min(n, target)
    while n % t:
        t //= 2
    return t


def kernel(x_prompt, x_sample, c_prompt, c_sample, cache_k, cache_v, state_ssm_re, state_ssm_im, page_table, w_cond, b_cond, w_in, sb_bias, ssm_a_re, ssm_a_im, ssm_log_dt, ssm_b_re, ssm_b_im, ssm_c_re, ssm_c_im, ssm_d, w_glu, b_glu, w_att_out, w_ssm_out, w_out, ln_g, ln_b):
    depth = w_in.shape[0]
    assert depth == 1, "single-layer trunk"
    bsz, seq, d = x_prompt.shape
    dbsz, dseq, _ = x_sample.shape
    assert dseq == 1 and w_in.shape[2] == 8 * d
    n_heads = sb_bias.shape[-1]
    hd = cache_k.shape[-1]
    n_groups, n_state = ssm_a_re.shape[1:]
    alpha = (2.0 * depth) ** 0.25
    l = 0

    mod = _modulation(jnp.concatenate([c_prompt, c_sample], axis=0), w_cond[l], b_cond[l])
    shift, scale, gate = mod[:, :d], mod[:, d:2 * d], mod[:, 2 * d:]
    row3 = lambda a, lo, hi, shape: a[lo:hi].reshape(shape)
    p_shape, s_shape = (bsz, 1, d), (1, dbsz, d)

    ab_re, ab_im, bb_re, bb_im = _ssm_discretize(ssm_a_re[l], ssm_a_im[l], ssm_log_dt[l],
                                                 ssm_b_re[l], ssm_b_im[l])
    gpb = GROUPS_PER_BLOCK
    nblk = n_groups // gpb
    bmat = jnp.concatenate([_block_diag(bb_re, gpb), _block_diag(bb_im, gpb)], axis=-1)
    cmat = jnp.concatenate([_block_diag(ssm_c_re[l].transpose(0, 2, 1), gpb),
                            _block_diag(-ssm_c_im[l].transpose(0, 2, 1), gpb)], axis=1)
    a_blk = lambda a: a.reshape(nblk, 1, gpb * n_state)
    d_skip = ssm_d[l].reshape(1, d)

    w = w_in[l]
    wt = w[:, :3 * d].reshape(d, 3, d).transpose(1, 2, 0)
    wt_kvq = jnp.stack([wt[1], wt[2], wt[0]])

    kt_p, vt_p, z_p = _inproj_prompt(x_prompt, row3(scale, 0, bsz, p_shape), row3(shift, 0, bsz, p_shape),
                                     w.astype(BF16), wt_kvq[:2].astype(BF16),
                                     tm=_pick(seq, 512), cols=(0, 3, 4, 5, 6, 7))
    x_s = x_sample.reshape(dbsz, d)
    kvqt_s, z_s = _inproj_sample(x_s, scale[bsz:], shift[bsz:], w, wt_kvq)
    z_s = z_s.reshape(1, dbsz, 5 * d)

    pool = lambda c: c[l].transpose(0, 2, 3, 1)
    o_p, o_s = _attention(z_p, kt_p, vt_p, sb_bias[l], kvqt_s[2], pool(cache_k), pool(cache_v), page_table,
                          d=d, hd=hd, tq=_pick(seq, 256), nh=4)

    y_p, sr_p, si_p = _ssm_prompt(z_p, 2, bmat, a_blk(ab_re), a_blk(ab_im), cmat, d_skip, tt=_pick(seq, 256))
    out_p = _post(x_prompt, row3(gate, 0, bsz, p_shape), y_p, o_p, z_p, (1, 3, 4, 5),
                  w_glu[l], b_glu[l], w_att_out[l], w_ssm_out[l], w_out[l], ln_g[l], ln_b[l],
                  tm=_pick(seq, 512), alpha=alpha)
    y_s, sr_s, si_s = _ssm_scan(z_s[:, :, d:2 * d], state_ssm_re[l].reshape(dbsz, -1),
                                state_ssm_im[l].reshape(dbsz, -1), bmat, a_blk(ab_re), a_blk(ab_im),
                                cmat, d_skip, tt=1, precise=True)
    out_s = _post(x_s.reshape(1, dbsz, d), row3(gate, bsz, bsz + dbsz, s_shape), y_s, o_s.reshape(1, dbsz, d),
                  z_s, (0, 2, 3, 4),
                  w_glu[l], b_glu[l], w_att_out[l], w_ssm_out[l], w_out[l], ln_g[l], ln_b[l],
                  tm=dbsz, alpha=alpha)

    heads_t = lambda t, b, s: t.reshape(1, b, n_heads, hd, s).transpose(0, 1, 4, 2, 3)
    state = lambda t, b: t.reshape(1, b, n_groups, n_state)
    return (out_p, out_s.reshape(dbsz, 1, d),
            heads_t(kt_p, bsz, seq), heads_t(vt_p, bsz, seq), state(sr_p, bsz), state(si_p, bsz),
            heads_t(kvqt_s[0], 1, dbsz).reshape(1, dbsz, 1, n_heads, hd),
            heads_t(kvqt_s[1], 1, dbsz).reshape(1, dbsz, 1, n_heads, hd),
            state(sr_s, dbsz), state(si_s, dbsz))
```
